```python
import jax, jax.numpy as jnp
from jax import lax
import numpy as np

D_MODEL = 1024
BATCH = 4
SEQ = 8192
DEPTH = 1

CHUNK = 64
LEFT_CHUNKS = 8
BAND = (LEFT_CHUNKS + 1) * CHUNK
N_HEADS = 8
HEAD_DIM = 64
D_ATTN = N_HEADS * HEAD_DIM
REL_CLIP = 128
D_CONV = 512
CONV_WIDTH = 31
N_BRANCH = 2
D_IN = 3 * D_ATTN + 2 * D_CONV + N_BRANCH * D_MODEL
D_FF = 2816
EPS = 1e-6

kernel_name = "hybrid_chunked_attn_conformer_conv_macaron"


def rms_norm(x, g):
    xf = x.astype(jnp.float32)
    y = xf * lax.rsqrt(jnp.mean(xf * xf, axis=-1, keepdims=True) + EPS)
    return (y * g.astype(jnp.float32)).astype(x.dtype)


def layer_norm(x, g, b):
    xf = x.astype(jnp.float32)
    mu = jnp.mean(xf, axis=-1, keepdims=True)
    var = jnp.mean(jnp.square(xf - mu), axis=-1, keepdims=True)
    y = (xf - mu) * lax.rsqrt(var + EPS)
    return (y * g.astype(jnp.float32) + b.astype(jnp.float32)).astype(x.dtype)


def swiglu(x, w_gate, w_up, w_down):
    return (jax.nn.silu(x @ w_gate) * (x @ w_up)) @ w_down


def chunked_attention(q, k, v, rel_table):
    B, S, H, Dh = q.shape
    nc = S // CHUNK
    qc = q.reshape(B, nc, CHUNK, H, Dh)
    pad = ((0, 0), (LEFT_CHUNKS, 0), (0, 0), (0, 0), (0, 0))
    kp = jnp.pad(k.reshape(B, nc, CHUNK, H, Dh), pad)
    vp = jnp.pad(v.reshape(B, nc, CHUNK, H, Dh), pad)
    k_band = jnp.concatenate([kp[:, j:j + nc] for j in range(LEFT_CHUNKS + 1)], axis=2)
    v_band = jnp.concatenate([vp[:, j:j + nc] for j in range(LEFT_CHUNKS + 1)], axis=2)
    scores = jnp.einsum('bnqhd,bnkhd->bnhqk', qc, k_band).astype(jnp.float32) * (HEAD_DIM ** -0.5)
    qi = jnp.arange(CHUNK)[:, None]
    kj = jnp.arange(BAND)[None, :]
    dist = qi + LEFT_CHUNKS * CHUNK - kj
    idx = jnp.clip(dist, -REL_CLIP, REL_CLIP) + REL_CLIP
    bias = rel_table.astype(jnp.float32)[:, idx]
    scores = scores + bias[None, None]
    key_chunk = jnp.arange(nc)[:, None] + (jnp.arange(BAND) // CHUNK)[None, :] - LEFT_CHUNKS
    valid = (key_chunk >= 0)[None, :, None, None, :]
    scores = jnp.where(valid, scores, jnp.float32(-1e30))
    p = jax.nn.softmax(scores, axis=-1).astype(v.dtype)
    out = jnp.einsum('bnhqk,bnkhd->bnqhd', p, v_band)
    return out.reshape(B, S, H * Dh)


def conv_module(c_in, glu_bias, dw_w, dw_b, ln_g, ln_b, w_out):
    c = c_in + glu_bias
    c = c[..., :D_CONV] * jax.nn.sigmoid(c[..., D_CONV:])
    c = lax.conv_general_dilated(
        c, dw_w.astype(c.dtype), window_strides=(1,), padding=((CONV_WIDTH - 1, 0),),
        dimension_numbers=('NWC', 'WIO', 'NWC'), feature_group_count=D_CONV) + dw_b
    c = jax.nn.silu(layer_norm(c, ln_g, ln_b))
    return c @ w_out


def hybrid_mixer(u, w_in, gate_bias, rel_table, w_attn_out, conv_glu_bias, conv_dw_w,
                 conv_dw_b, conv_ln_g, conv_ln_b, conv_w_out, w_out):
    B, S, _ = u.shape
    proj = u @ w_in
    q, k, v, c_in, g = jnp.split(
        proj, [D_ATTN, 2 * D_ATTN, 3 * D_ATTN, 3 * D_ATTN + 2 * D_CONV], axis=-1)
    shp = (B, S, N_HEADS, HEAD_DIM)
    y_a = chunked_attention(q.reshape(shp), k.reshape(shp), v.reshape(shp), rel_table) @ w_attn_out
    y_b = conv_module(c_in, conv_glu_bias, conv_dw_w, conv_dw_b, conv_ln_g, conv_ln_b, conv_w_out)
    gates = jax.nn.sigmoid(g + gate_bias)
    merged = gates[..., :D_MODEL] * y_a + gates[..., D_MODEL:] * y_b
    return merged @ w_out


def setup_inputs(seed: int = 0) -> dict:
    key = jax.random.key(seed)
    ks = jax.random.split(key, 32)
    L, D = DEPTH, D_MODEL

    def w(k, shape, fan_in):
        return jax.random.normal(k, shape, jnp.float32) * (fan_in ** -0.5)

    def gain(k, shape):
        return 1.0 + 0.05 * jax.random.normal(k, shape, jnp.float32)

    def small(k, shape, s=0.02):
        return s * jax.random.normal(k, shape, jnp.float32)

    return {
        "x": jax.random.normal(ks[0], (BATCH, SEQ, D), jnp.float32),
        "ffn1_norm_pre": gain(ks[1], (L, D)),
        "ffn1_w_gate": w(ks[2], (L, D, D_FF), D),
        "ffn1_w_up": w(ks[3], (L, D, D_FF), D),
        "ffn1_w_down": w(ks[4], (L, D_FF, D), D_FF),
        "ffn1_norm_post": gain(ks[5], (L, D)),
        "mix_norm_pre": gain(ks[6], (L, D)),
        "w_in": w(ks[7], (L, D, D_IN), D),
        "gate_bias": small(ks[8], (L, N_BRANCH * D), 0.1),
        "rel_table": small(ks[9], (L, N_HEADS, 2 * REL_CLIP + 1), 0.5),
        "w_attn_out": w(ks[10], (L, D_ATTN, D), D_ATTN),
        "conv_glu_bias": small(ks[11], (L, 2 * D_CONV)),
        "conv_dw_w": w(ks[12], (L, CONV_WIDTH, 1, D_CONV), CONV_WIDTH),
        "conv_dw_b": small(ks[13], (L, D_CONV)),
        "conv_ln_g": gain(ks[14], (L, D_CONV)),
        "conv_ln_b": small(ks[15], (L, D_CONV)),
        "conv_w_out": w(ks[16], (L, D_CONV, D), D_CONV),
        "w_out": w(ks[17], (L, D, D), D),
        "mix_norm_post": gain(ks[18], (L, D)),
        "ffn2_norm_pre": gain(ks[19], (L, D)),
        "ffn2_w_gate": w(ks[20], (L, D, D_FF), D),
        "ffn2_w_up": w(ks[21], (L, D, D_FF), D),
        "ffn2_w_down": w(ks[22], (L, D_FF, D), D_FF),
        "ffn2_norm_post": gain(ks[23], (L, D)),
    }


def reference(x, ffn1_norm_pre, ffn1_w_gate, ffn1_w_up, ffn1_w_down, ffn1_norm_post,
              mix_norm_pre, w_in, gate_bias, rel_table, w_attn_out, conv_glu_bias,
              conv_dw_w, conv_dw_b, conv_ln_g, conv_ln_b, conv_w_out, w_out, mix_norm_post,
              ffn2_norm_pre, ffn2_w_gate, ffn2_w_up, ffn2_w_down, ffn2_norm_post):
    h = x
    for l in range(DEPTH):
        f = swiglu(rms_norm(h, ffn1_norm_pre[l]), ffn1_w_gate[l], ffn1_w_up[l], ffn1_w_down[l])
        h = h + 0.5 * rms_norm(f, ffn1_norm_post[l])
        m = hybrid_mixer(rms_norm(h, mix_norm_pre[l]), w_in[l], gate_bias[l], rel_table[l],
                         w_attn_out[l], conv_glu_bias[l], conv_dw_w[l], conv_dw_b[l],
                         conv_ln_g[l], conv_ln_b[l], conv_w_out[l], w_out[l])
        h = h + rms_norm(m, mix_norm_post[l])
        f = swiglu(rms_norm(h, ffn2_norm_pre[l]), ffn2_w_gate[l], ffn2_w_up[l], ffn2_w_down[l])
        h = h + 0.5 * rms_norm(f, ffn2_norm_post[l])
    return h
```

```python
import functools

import jax
import jax.numpy as jnp
from jax import lax
from jax.experimental import pallas as pl
from jax.experimental.pallas import tpu as pltpu

CHUNK = 64
LEFT_CHUNKS = 8
N_HEADS = 8
HEAD_DIM = 64
D_ATTN = N_HEADS * HEAD_DIM
REL_CLIP = 128
D_CONV = 512
CONV_WIDTH = 31
EPS = 1e-6
NEG_INF = -1e30

LANES = 128
VMEM_LIMIT_BYTES = 56 * 1024 * 1024

ROW_TILE = LEFT_CHUNKS * CHUNK
Q_BLOCK = 4 * CHUNK
K_BLOCK = Q_BLOCK + LEFT_CHUNKS * CHUNK
CONV_HALO = 32
CONV_ROWS = 128

BF16 = jnp.bfloat16
F32 = jnp.float32


def _rms(x, g):
    ms = jnp.mean(x * x, axis=-1, keepdims=True)
    return x * lax.rsqrt(ms + EPS) * g


def _const_spec(shape):
    nd = len(shape)
    return pl.BlockSpec(shape, lambda i: (0,) * nd, pipeline_mode=pl.Buffered(1))


def _row_spec(rows, cols):
    return pl.BlockSpec((rows, cols), lambda i: (i, 0))


def _ffn_kernel(x_ref, gpre_ref, wg_ref, wu_ref, wd_ref, gpost_ref, o_ref):
    x = x_ref[...]
    xn = _rms(x, gpre_ref[...]).astype(BF16)
    g = jnp.dot(xn, wg_ref[...], preferred_element_type=F32)
    u = jnp.dot(xn, wu_ref[...], preferred_element_type=F32)
    a = (g * jax.nn.sigmoid(g) * u).astype(BF16)
    f = jnp.dot(a, wd_ref[...], preferred_element_type=F32)
    o_ref[...] = x + 0.5 * _rms(f, gpost_ref[...])


def _ffn(x, gpre, wg, wu, wd, gpost):
    t, d = x.shape
    dff = wg.shape[1]
    return pl.pallas_call(
        _ffn_kernel,
        grid=(t // ROW_TILE,),
        in_specs=[
            _row_spec(ROW_TILE, d),
            _const_spec((1, d)),
            _const_spec((d, dff)),
            _const_spec((d, dff)),
            _const_spec((dff, d)),
            _const_spec((1, d)),
        ],
        out_specs=_row_spec(ROW_TILE, d),
        out_shape=jax.ShapeDtypeStruct((t, d), F32),
        compiler_params=pltpu.CompilerParams(
            dimension_semantics=("arbitrary",), vmem_limit_bytes=VMEM_LIMIT_BYTES),
        name="ffn",
    )(x, gpre, wg, wu, wd, gpost)


def _proj_kernel(h_ref, g_ref, win_ref, glub_ref, gateb_ref,
                 q_ref, k_ref, v_ref, c_ref, gate_ref):
    u = _rms(h_ref[...], g_ref[...]).astype(BF16)

    def seg(lo, hi):
        return jnp.dot(u, win_ref[:, lo:hi], preferred_element_type=F32)

    a = D_ATTN
    q_ref[...] = (seg(0, a) * (HEAD_DIM ** -0.5)).astype(BF16)
    k_ref[...] = seg(a, 2 * a).astype(BF16)
    v_ref[...] = seg(2 * a, 3 * a).astype(BF16)
    c0 = 3 * a
    ca = seg(c0, c0 + D_CONV) + glub_ref[:, :D_CONV]
    cb = seg(c0 + D_CONV, c0 + 2 * D_CONV) + glub_ref[:, D_CONV:]
    c_ref[...] = (ca * jax.nn.sigmoid(cb)).astype(BF16)
    g0 = c0 + 2 * D_CONV
    gate_ref[...] = jax.nn.sigmoid(seg(g0, win_ref.shape[1]) + gateb_ref[...]).astype(BF16)


def _proj(h, g, win, glub, gateb):
    t, d = h.shape
    din = win.shape[1]
    ngate = gateb.shape[1]
    outs = [jax.ShapeDtypeStruct((t, D_ATTN), BF16)] * 3 + [
        jax.ShapeDtypeStruct((t, D_CONV), BF16), jax.ShapeDtypeStruct((t, ngate), BF16)]
    return pl.pallas_call(
        _proj_kernel,
        grid=(t // ROW_TILE,),
        in_specs=[
            _row_spec(ROW_TILE, d),
            _const_spec((1, d)),
            _const_spec((d, din)),
            _const_spec((1, 2 * D_CONV)),
            _const_spec((1, ngate)),
        ],
        out_specs=[_row_spec(ROW_TILE, D_ATTN)] * 3 + [
            _row_spec(ROW_TILE, D_CONV), _row_spec(ROW_TILE, ngate)],
        out_shape=outs,
        compiler_params=pltpu.CompilerParams(
            dimension_semantics=("arbitrary",), vmem_limit_bytes=VMEM_LIMIT_BYTES),
        name="proj",
    )(h, g, win, glub, gateb)


def _mixer_kernel(tiles_per_seq,
                  q_ref, kp_ref, kc_ref, vp_ref, vc_ref, cp_ref, cc_ref, gate_ref, h_ref,
                  bias_ref, wa_ref, dww_ref, dwb_ref, lng_ref, lnb_ref, wc_ref, wo_ref, gpost_ref,
                  o_ref, attn_ref, cbuf_ref, conv_ref):
    tm = q_ref.shape[0]
    d = h_ref.shape[1]
    first = (pl.program_id(0) % tiles_per_seq) == 0

    lane = lax.broadcasted_iota(jnp.int32, (1, LANES), 1)
    lo_half = lane < HEAD_DIM
    kidx = lax.broadcasted_iota(jnp.int32, (1, K_BLOCK), 1)
    for hp in range(N_HEADS // 2):
        sl = slice(LANES * hp, LANES * (hp + 1))
        k2 = jnp.concatenate([kp_ref[:, sl], kc_ref[:, sl]], axis=0)
        v2 = jnp.concatenate([vp_ref[:, sl], vc_ref[:, sl]], axis=0)
        q2 = q_ref[:, sl]
        pair = []
        for hh in range(2):
            mine = lo_half if hh == 0 else jnp.logical_not(lo_half)
            kz = jnp.where(mine, k2, jnp.zeros_like(k2))
            blocks = []
            for r0 in range(0, tm, Q_BLOCK):
                s = lax.dot_general(q2[r0:r0 + Q_BLOCK], kz[r0:r0 + K_BLOCK],
                                    (((1,), (1,)), ((), ())), preferred_element_type=F32)
                kmask = jnp.where(jnp.logical_and(first, kidx + r0 < tm), NEG_INF, 0.0)
                s = s + bias_ref[2 * hp + hh] + kmask
                m = jnp.max(s, axis=-1, keepdims=True)
                p = jnp.exp(s - m)
                l = jnp.sum(p, axis=-1, keepdims=True)
                o = jnp.dot(p.astype(BF16), v2[r0:r0 + K_BLOCK], preferred_element_type=F32)
                blocks.append(o / l)
            pair.append(jnp.concatenate(blocks, axis=0))
        attn_ref[:, sl] = jnp.where(lo_half, pair[0], pair[1]).astype(BF16)
    y_a = jnp.dot(attn_ref[...], wa_ref[...], preferred_element_type=F32)

    hist = jnp.where(first, 0.0, cp_ref[...].astype(F32))
    cbuf_ref[0:CONV_HALO, :] = hist
    cbuf_ref[CONV_HALO:CONV_HALO + tm, :] = cc_ref[...].astype(F32)
    base = CONV_HALO - (CONV_WIDTH - 1)
    for c0 in range(0, D_CONV, LANES):
        cs = slice(c0, c0 + LANES)
        for r0 in range(0, tm, CONV_ROWS):
            acc = jnp.zeros((CONV_ROWS, LANES), F32) + dwb_ref[:, cs]
            for j in range(CONV_WIDTH):
                acc = acc + cbuf_ref[pl.ds(r0 + base + j, CONV_ROWS), cs] * dww_ref[j:j + 1, cs]
            conv_ref[r0:r0 + CONV_ROWS, cs] = acc
    cv = conv_ref[...]
    mu = jnp.mean(cv, axis=-1, keepdims=True)
    xc = cv - mu
    var = jnp.mean(xc * xc, axis=-1, keepdims=True)
    ln = xc * lax.rsqrt(var + EPS) * lng_ref[...] + lnb_ref[...]
    act = (ln * jax.nn.sigmoid(ln)).astype(BF16)
    y_b = jnp.dot(act, wc_ref[...], preferred_element_type=F32)

    merged = (gate_ref[:, :d].astype(F32) * y_a + gate_ref[:, d:].astype(F32) * y_b).astype(BF16)
    out = jnp.dot(merged, wo_ref[...], preferred_element_type=F32)
    o_ref[...] = h_ref[...] + _rms(out, gpost_ref[...])


def _mixer(q, k, v, c, gates, h, bias, wa, dww, dwb, lng, lnb, wc, wo, gpost, seq_len):
    t, d = h.shape
    tm = ROW_TILE
    tiles_per_seq = seq_len // tm
    halo_per_tile = tm // CONV_HALO
    prev = lambda i: (jnp.maximum(i - 1, 0), 0)
    halo = lambda i: (jnp.maximum(i * halo_per_tile - 1, 0), 0)
    return pl.pallas_call(
        functools.partial(_mixer_kernel, tiles_per_seq),
        grid=(t // tm,),
        in_specs=[
            _row_spec(tm, D_ATTN),
            pl.BlockSpec((tm, D_ATTN), prev),
            _row_spec(tm, D_ATTN),
            pl.BlockSpec((tm, D_ATTN), prev),
            _row_spec(tm, D_ATTN),
            pl.BlockSpec((CONV_HALO, D_CONV), halo),
            _row_spec(tm, D_CONV),
            _row_spec(tm, 2 * d),
            _row_spec(tm, d),
            _const_spec(bias.shape),
            _const_spec(wa.shape),
            _const_spec(dww.shape),
            _const_spec(dwb.shape),
            _const_spec(lng.shape),
            _const_spec(lnb.shape),
            _const_spec(wc.shape),
            _const_spec(wo.shape),
            _const_spec(gpost.shape),
        ],
        out_specs=_row_spec(tm, d),
        out_shape=jax.ShapeDtypeStruct((t, d), F32),
        scratch_shapes=[
            pltpu.VMEM((tm, D_ATTN), BF16),
            pltpu.VMEM((CONV_HALO + tm, D_CONV), F32),
            pltpu.VMEM((tm, D_CONV), F32),
        ],
        compiler_params=pltpu.CompilerParams(
            dimension_semantics=("arbitrary",), vmem_limit_bytes=VMEM_LIMIT_BYTES),
        name="mixer",
    )(q, k, k, v, v, c, c, gates, h, bias, wa, dww, dwb, lng, lnb, wc, wo, gpost)


def _rel_bias(rel_table):
    r = jnp.arange(Q_BLOCK)[:, None]
    k = jnp.arange(K_BLOCK)[None, :]
    rel = k - CHUNK * (r // CHUNK)
    valid = jnp.logical_and(rel >= 0, rel < (LEFT_CHUNKS + 1) * CHUNK)
    idx = jnp.clip(r + LEFT_CHUNKS * CHUNK - k, -REL_CLIP, REL_CLIP) + REL_CLIP
    return jnp.where(valid[None], rel_table.astype(F32)[:, idx], NEG_INF)


def kernel(x, ffn1_norm_pre, ffn1_w_gate, ffn1_w_up, ffn1_w_down, ffn1_norm_post, mix_norm_pre, w_in, gate_bias, rel_table, w_attn_out, conv_glu_bias, conv_dw_w, conv_dw_b, conv_ln_g, conv_ln_b, conv_w_out, w_out, mix_norm_post, ffn2_norm_pre, ffn2_w_gate, ffn2_w_up, ffn2_w_down, ffn2_norm_post):
    b, s, d = x.shape
    assert s % ROW_TILE == 0 and d % LANES == 0
    h = x.reshape(b * s, d)
    depth = w_in.shape[0]
    row = lambda p: p.reshape(1, -1)
    for l in range(depth):
        h = _ffn(h, row(ffn1_norm_pre[l]), ffn1_w_gate[l].astype(BF16), ffn1_w_up[l].astype(BF16),
                 ffn1_w_down[l].astype(BF16), row(ffn1_norm_post[l]))
        q, k, v, c, gates = _proj(h, row(mix_norm_pre[l]), w_in[l].astype(BF16),
                                  row(conv_glu_bias[l]), row(gate_bias[l]))
        h = _mixer(q, k, v, c, gates, h, _rel_bias(rel_table[l]), w_attn_out[l].astype(BF16),
                   conv_dw_w[l].reshape(CONV_WIDTH, D_CONV), row(conv_dw_b[l]), row(conv_ln_g[l]),
                   row(conv_ln_b[l]), conv_w_out[l].astype(BF16), w_out[l].astype(BF16),
                   row(mix_norm_post[l]), s)
        h = _ffn(h, row(ffn2_norm_pre[l]), ffn2_w_gate[l].astype(BF16), ffn2_w_up[l].astype(BF16),
                 ffn2_w_down[l].astype(BF16), row(ffn2_norm_post[l]))
    return h.reshape(b, s, d)
```

```python
import functools

import jax
import jax.numpy as jnp
from jax import lax
from jax.experimental import pallas as pl
from jax.experimental.pallas import tpu as pltpu

CHUNK = 64
LEFT_CHUNKS = 8
N_HEADS = 8
HEAD_DIM = 64
D_ATTN = N_HEADS * HEAD_DIM
REL_CLIP = 128
D_CONV = 512
CONV_WIDTH = 31
EPS = 1e-6
NEG_INF = -1e30
LOG2E = 1.4426950408889634

LANES = 128
VMEM_LIMIT_BYTES = 56 * 1024 * 1024

ROW_TILE = LEFT_CHUNKS * CHUNK
Q_BLOCK = 4 * CHUNK
K_BLOCK = Q_BLOCK + LEFT_CHUNKS * CHUNK
CONV_HALO = 32
CONV_ROWS = 128
SUBLANES = 8
BIAS_ROW_LEN = Q_BLOCK + K_BLOCK

BF16 = jnp.bfloat16
F32 = jnp.float32


def _rms(x, g):
    ms = jnp.mean(x * x, axis=-1, keepdims=True)
    return x * lax.rsqrt(ms + EPS) * g


def _const_spec(shape):
    nd = len(shape)
    return pl.BlockSpec(shape, lambda i: (0,) * nd, pipeline_mode=pl.Buffered(1))


def _row_spec(rows, cols):
    return pl.BlockSpec((rows, cols), lambda i: (i, 0))


def _ffn_kernel(x_ref, gpre_ref, wg_ref, wu_ref, wd_ref, gpost_ref, o_ref):
    x = x_ref[...]
    xn = _rms(x, gpre_ref[...]).astype(BF16)
    g = jnp.dot(xn, wg_ref[...], preferred_element_type=F32)
    u = jnp.dot(xn, wu_ref[...], preferred_element_type=F32)
    a = (g * jax.nn.sigmoid(g) * u).astype(BF16)
    f = jnp.dot(a, wd_ref[...], preferred_element_type=F32)
    o_ref[...] = x + 0.5 * _rms(f, gpost_ref[...])


def _ffn(x, gpre, wg, wu, wd, gpost):
    t, d = x.shape
    dff = wg.shape[1]
    return pl.pallas_call(
        _ffn_kernel,
        grid=(t // ROW_TILE,),
        in_specs=[
            _row_spec(ROW_TILE, d),
            _const_spec((1, d)),
            _const_spec((d, dff)),
            _const_spec((d, dff)),
            _const_spec((dff, d)),
            _const_spec((1, d)),
        ],
        out_specs=_row_spec(ROW_TILE, d),
        out_shape=jax.ShapeDtypeStruct((t, d), F32),
        compiler_params=pltpu.CompilerParams(
            dimension_semantics=("arbitrary",), vmem_limit_bytes=VMEM_LIMIT_BYTES),
        name="ffn",
    )(x, gpre, wg, wu, wd, gpost)


def _proj_kernel(h_ref, g_ref, win_ref, glub_ref, gateb_ref,
                 q_ref, k_ref, v_ref, c_ref, gate_ref):
    u = _rms(h_ref[...], g_ref[...]).astype(BF16)

    def seg(lo, hi):
        return jnp.dot(u, win_ref[:, lo:hi], preferred_element_type=F32)

    a = D_ATTN
    q_ref[...] = (seg(0, a) * (HEAD_DIM ** -0.5 * LOG2E)).astype(BF16)
    k_ref[...] = seg(a, 2 * a).astype(BF16)
    v_ref[...] = seg(2 * a, 3 * a).astype(BF16)
    c0 = 3 * a
    ca = seg(c0, c0 + D_CONV) + glub_ref[:, :D_CONV]
    cb = seg(c0 + D_CONV, c0 + 2 * D_CONV) + glub_ref[:, D_CONV:]
    c_ref[...] = (ca * jax.nn.sigmoid(cb)).astype(BF16)
    g0 = c0 + 2 * D_CONV
    gate_ref[...] = jax.nn.sigmoid(seg(g0, win_ref.shape[1]) + gateb_ref[...]).astype(BF16)


def _proj(h, g, win, glub, gateb):
    t, d = h.shape
    din = win.shape[1]
    ngate = gateb.shape[1]
    outs = [jax.ShapeDtypeStruct((t, D_ATTN), BF16)] * 3 + [
        jax.ShapeDtypeStruct((t, D_CONV), BF16), jax.ShapeDtypeStruct((t, ngate), BF16)]
    return pl.pallas_call(
        _proj_kernel,
        grid=(t // ROW_TILE,),
        in_specs=[
            _row_spec(ROW_TILE, d),
            _const_spec((1, d)),
            _const_spec((d, din)),
            _const_spec((1, 2 * D_CONV)),
            _const_spec((1, ngate)),
        ],
        out_specs=[_row_spec(ROW_TILE, D_ATTN)] * 3 + [
            _row_spec(ROW_TILE, D_CONV), _row_spec(ROW_TILE, ngate)],
        out_shape=outs,
        compiler_params=pltpu.CompilerParams(
            dimension_semantics=("arbitrary",), vmem_limit_bytes=VMEM_LIMIT_BYTES),
        name="proj",
    )(h, g, win, glub, gateb)


def _mixer_kernel(tiles_per_seq,
                  q_ref, kp_ref, kc_ref, vp_ref, vc_ref, cp_ref, cc_ref, gate_ref, h_ref,
                  brow_ref, wa_ref, dww_ref, dwb_ref, lng_ref, lnb_ref, wc_ref, wo_ref, gpost_ref,
                  o_ref, bias_ref, attn_ref, cbuf_ref, shift_ref, conv_ref):
    tm = q_ref.shape[0]
    d = h_ref.shape[1]
    first = (pl.program_id(0) % tiles_per_seq) == 0

    @pl.when(pl.program_id(0) == 0)
    def _():
        r = lax.broadcasted_iota(jnp.int32, (Q_BLOCK, K_BLOCK), 0)
        k = lax.broadcasted_iota(jnp.int32, (Q_BLOCK, K_BLOCK), 1)
        rel = k - r + (r & (CHUNK - 1))
        in_band = jnp.logical_and(rel >= 0, rel < (LEFT_CHUNKS + 1) * CHUNK)
        for hd in range(N_HEADS):
            rows = jnp.broadcast_to(brow_ref[hd:hd + 1, :], (Q_BLOCK, BIAS_ROW_LEN))
            toep = pltpu.roll(rows, 0, 1, stride=1, stride_axis=0)
            bias_ref[hd] = jnp.where(in_band, toep[:, Q_BLOCK:], NEG_INF)

    lane = lax.broadcasted_iota(jnp.int32, (1, LANES), 1)
    lo_half = lane < HEAD_DIM
    krow = lax.broadcasted_iota(jnp.int32, (2 * tm, LANES), 0)
    klane = lax.broadcasted_iota(jnp.int32, (2 * tm, LANES), 1)
    kneg = jnp.where(jnp.logical_and(first, krow < tm), NEG_INF, 0.0)
    for hp in range(N_HEADS // 2):
        sl = slice(LANES * hp, LANES * (hp + 1))
        k2 = jnp.concatenate([kp_ref[:, sl], kc_ref[:, sl]], axis=0)
        v2 = jnp.concatenate([vp_ref[:, sl], vc_ref[:, sl]], axis=0)
        q2 = q_ref[:, sl]
        pair = []
        for hh in range(2):
            mine = lo_half if hh == 0 else jnp.logical_not(lo_half)
            spare = HEAD_DIM if hh == 0 else 0
            other = jnp.where(klane == spare, kneg, 0.0).astype(BF16)
            kz = jnp.where(mine, k2, other)
            qz = jnp.where(mine, q2, jnp.ones_like(q2))
            blocks = []
            for r0 in range(0, tm, Q_BLOCK):
                s = lax.dot_general(qz[r0:r0 + Q_BLOCK], kz[r0:r0 + K_BLOCK],
                                    (((1,), (1,)), ((), ())), preferred_element_type=F32)
                s = s + bias_ref[2 * hp + hh]
                m = jnp.max(s, axis=-1, keepdims=True)
                p = jnp.exp2(s - m)
                l = jnp.sum(p, axis=-1, keepdims=True)
                o = jnp.dot(p.astype(BF16), v2[r0:r0 + K_BLOCK], preferred_element_type=F32)
                blocks.append(o / l)
            pair.append(jnp.concatenate(blocks, axis=0))
        attn_ref[:, sl] = jnp.where(lo_half, pair[0], pair[1]).astype(BF16)
    y_a = jnp.dot(attn_ref[...], wa_ref[...], preferred_element_type=F32)

    hist = jnp.where(first, 0.0, cp_ref[...].astype(F32))
    cbuf_ref[0:CONV_HALO, :] = hist
    cbuf_ref[CONV_HALO:CONV_HALO + tm, :] = cc_ref[...].astype(F32)
    base = CONV_HALO - (CONV_WIDTH - 1)
    span = CONV_HALO + tm - SUBLANES
    for c0 in range(0, D_CONV, LANES):
        cs = slice(c0, c0 + LANES)
        for sh in range(1, SUBLANES):
            shift_ref[sh, 0:span, :] = cbuf_ref[pl.ds(sh, span), cs]
        for r0 in range(0, tm, CONV_ROWS):
            acc = jnp.zeros((CONV_ROWS, LANES), F32) + dwb_ref[:, cs]
            for j in range(CONV_WIDTH):
                al, sh = divmod(base + j, SUBLANES)
                if sh == 0:
                    tap = cbuf_ref[pl.ds(r0 + al * SUBLANES, CONV_ROWS), cs]
                else:
                    tap = shift_ref[sh, pl.ds(r0 + al * SUBLANES, CONV_ROWS), :]
                acc = acc + tap * dww_ref[j:j + 1, cs]
            conv_ref[r0:r0 + CONV_ROWS, cs] = acc
    cv = conv_ref[...]
    mu = jnp.mean(cv, axis=-1, keepdims=True)
    xc = cv - mu
    var = jnp.mean(xc * xc, axis=-1, keepdims=True)
    ln = xc * lax.rsqrt(var + EPS) * lng_ref[...] + lnb_ref[...]
    act = (ln * jax.nn.sigmoid(ln)).astype(BF16)
    y_b = jnp.dot(act, wc_ref[...], preferred_element_type=F32)

    merged = (gate_ref[:, :d].astype(F32) * y_a + gate_ref[:, d:].astype(F32) * y_b).astype(BF16)
    out = jnp.dot(merged, wo_ref[...], preferred_element_type=F32)
    o_ref[...] = h_ref[...] + _rms(out, gpost_ref[...])


def _mixer(q, k, v, c, gates, h, brow, wa, dww, dwb, lng, lnb, wc, wo, gpost, seq_len):
    t, d = h.shape
    tm = ROW_TILE
    tiles_per_seq = seq_len // tm
    halo_per_tile = tm // CONV_HALO
    prev = lambda i: (jnp.maximum(i - 1, 0), 0)
    halo = lambda i: (jnp.maximum(i * halo_per_tile - 1, 0), 0)
    return pl.pallas_call(
        functools.partial(_mixer_kernel, tiles_per_seq),
        grid=(t // tm,),
        in_specs=[
            _row_spec(tm, D_ATTN),
            pl.BlockSpec((tm, D_ATTN), prev),
            _row_spec(tm, D_ATTN),
            pl.BlockSpec((tm, D_ATTN), prev),
            _row_spec(tm, D_ATTN),
            pl.BlockSpec((CONV_HALO, D_CONV), halo),
            _row_spec(tm, D_CONV),
            _row_spec(tm, 2 * d),
            _row_spec(tm, d),
            _const_spec(brow.shape),
            _const_spec(wa.shape),
            _const_spec(dww.shape),
            _const_spec(dwb.shape),
            _const_spec(lng.shape),
            _const_spec(lnb.shape),
            _const_spec(wc.shape),
            _const_spec(wo.shape),
            _const_spec(gpost.shape),
        ],
        out_specs=_row_spec(tm, d),
        out_shape=jax.ShapeDtypeStruct((t, d), F32),
        scratch_shapes=[
            pltpu.VMEM((N_HEADS, Q_BLOCK, K_BLOCK), F32),
            pltpu.VMEM((tm, D_ATTN), BF16),
            pltpu.VMEM((CONV_HALO + tm, D_CONV), F32),
            pltpu.VMEM((SUBLANES, CONV_HALO + tm, LANES), F32),
            pltpu.VMEM((tm, D_CONV), F32),
        ],
        compiler_params=pltpu.CompilerParams(
            dimension_semantics=("arbitrary",), vmem_limit_bytes=VMEM_LIMIT_BYTES),
        name="mixer",
    )(q, k, k, v, v, c, c, gates, h, brow, wa, dww, dwb, lng, lnb, wc, wo, gpost)


def _rel_bias_rows(rel_table):
    look_back = LEFT_CHUNKS * CHUNK
    n_far = look_back + Q_BLOCK - REL_CLIP
    n_near = BIAS_ROW_LEN - n_far - (2 * REL_CLIP + 1)
    t = rel_table.astype(F32)
    hd = t.shape[0]
    row = jnp.concatenate([jnp.broadcast_to(t[:, -1:], (hd, n_far)), t[:, ::-1],
                           jnp.broadcast_to(t[:, :1], (hd, n_near))], axis=1)
    return row * LOG2E


def kernel(x, ffn1_norm_pre, ffn1_w_gate, ffn1_w_up, ffn1_w_down, ffn1_norm_post, mix_norm_pre, w_in, gate_bias, rel_table, w_attn_out, conv_glu_bias, conv_dw_w, conv_dw_b, conv_ln_g, conv_ln_b, conv_w_out, w_out, mix_norm_post, ffn2_norm_pre, ffn2_w_gate, ffn2_w_up, ffn2_w_down, ffn2_norm_post):
    b, s, d = x.shape
    assert s % ROW_TILE == 0 and d % LANES == 0
    h = x.reshape(b * s, d)
    depth = w_in.shape[0]
    row = lambda p: p.reshape(1, -1)
    for l in range(depth):
        h = _ffn(h, row(ffn1_norm_pre[l]), ffn1_w_gate[l].astype(BF16), ffn1_w_up[l].astype(BF16),
                 ffn1_w_down[l].astype(BF16), row(ffn1_norm_post[l]))
        q, k, v, c, gates = _proj(h, row(mix_norm_pre[l]), w_in[l].astype(BF16),
                                  row(conv_glu_bias[l]), row(gate_bias[l]))
        h = _mixer(q, k, v, c, gates, h, _rel_bias_rows(rel_table[l]), w_attn_out[l].astype(BF16),
                   conv_dw_w[l].reshape(CONV_WIDTH, D_CONV), row(conv_dw_b[l]), row(conv_ln_g[l]),
                   row(conv_ln_b[l]), conv_w_out[l].astype(BF16), w_out[l].astype(BF16),
                   row(mix_norm_post[l]), s)
        h = _ffn(h, row(ffn2_norm_pre[l]), ffn2_w_gate[l].astype(BF16), ffn2_w_up[l].astype(BF16),
                 ffn2_w_down[l].astype(BF16), row(ffn2_norm_post[l]))
    return h.reshape(b, s, d)
```

```python
import functools

import jax
import jax.numpy as jnp
from jax import lax
from jax.experimental import pallas as pl
from jax.experimental.pallas import tpu as pltpu

CHUNK = 64
LEFT_CHUNKS = 8
N_HEADS = 8
HEAD_DIM = 64
D_ATTN = N_HEADS * HEAD_DIM
REL_CLIP = 128
D_CONV = 512
CONV_WIDTH = 31
EPS = 1e-6
NEG_INF = -1e30
LOG2E = 1.4426950408889634

LANES = 128
SUBLANES = 8
MXU_COLS = 256
VMEM_LIMIT_BYTES = 56 * 1024 * 1024

ROW_TILE = LEFT_CHUNKS * CHUNK
FFN_ROWS = 2 * ROW_TILE
Q_BLOCK = 4 * CHUNK
K_BLOCK = Q_BLOCK + LEFT_CHUNKS * CHUNK
SOFTMAX_KEYS = (LEFT_CHUNKS + 1) * CHUNK + (LANES - CHUNK)
BIAS_ROW_LEN = Q_BLOCK + K_BLOCK
CONV_HALO = 32
CONV_ROWS = 64
CONV_TILE_LAG = 3

BF16 = jnp.bfloat16
F32 = jnp.float32


def _rms(x, g):
    ms = jnp.mean(x * x, axis=-1, keepdims=True)
    return x * lax.rsqrt(ms + EPS) * g


def _mm(a, w):
    return lax.dot_general(a, w, (((1,), (0,)), ((), ())), preferred_element_type=F32)


def _params():
    return pltpu.CompilerParams(
        dimension_semantics=("arbitrary",), vmem_limit_bytes=VMEM_LIMIT_BYTES)


def _const_spec(shape):
    nd = len(shape)
    return pl.BlockSpec(shape, lambda i: (0,) * nd, pipeline_mode=pl.Buffered(1))


def _row_spec(rows, cols):
    return pl.BlockSpec((rows, cols), lambda i: (i, 0))


def _ffn_kernel(x_ref, gpre_ref, wg_ref, wu_ref, wd_ref, gpost_ref, o_ref):
    halves = [slice(r0, r0 + ROW_TILE) for r0 in range(0, FFN_ROWS, ROW_TILE)]
    xn = [_rms(x_ref[rows, :], gpre_ref[...]).astype(BF16) for rows in halves]
    act = []
    for xh in xn:
        g = _mm(xh, wg_ref[...])
        u = _mm(xh, wu_ref[...])
        act.append((g * jax.nn.sigmoid(g) * u).astype(BF16))
    f = [_mm(a, wd_ref[...]) for a in act]
    for rows, fh in zip(halves, f):
        o_ref[rows, :] = x_ref[rows, :] + 0.5 * _rms(fh, gpost_ref[...])


def _ffn(x, gpre, wg, wu, wd, gpost):
    t, d = x.shape
    dff = wg.shape[1]
    return pl.pallas_call(
        _ffn_kernel,
        grid=(t // FFN_ROWS,),
        in_specs=[
            _row_spec(FFN_ROWS, d),
            _const_spec((1, d)),
            _const_spec((d, dff)),
            _const_spec((d, dff)),
            _const_spec((dff, d)),
            _const_spec((1, d)),
        ],
        out_specs=_row_spec(FFN_ROWS, d),
        out_shape=jax.ShapeDtypeStruct((t, d), F32),
        compiler_params=_params(),
        name="ffn",
    )(x, gpre, wg, wu, wd, gpost)


def _proj_kernel(tiles_per_seq,
                 h_ref, g_ref, win_ref, glub_ref, gateb_ref, dww_ref, dwb_ref, lng_ref, lnb_ref,
                 q_ref, k_ref, v_ref, act_ref, gate_ref, w_ref, cbuf_ref, shift_ref, conv_ref):
    tm = h_ref.shape[0]

    @pl.when(pl.program_id(0) == 0)
    def _():
        w_ref[...] = win_ref[...].astype(BF16)

    u = _rms(h_ref[...], g_ref[...]).astype(BF16)

    def seg(lo, hi):
        return _mm(u, w_ref[:, lo:hi])

    @pl.when((pl.program_id(0) % tiles_per_seq) == 0)
    def _():
        cbuf_ref[0:CONV_HALO, :] = jnp.zeros((CONV_HALO, D_CONV), F32)

    a = D_ATTN
    c0 = 3 * a
    ca = seg(c0, c0 + D_CONV) + glub_ref[:, :D_CONV]
    cb = seg(c0 + D_CONV, c0 + 2 * D_CONV) + glub_ref[:, D_CONV:]
    cbuf_ref[CONV_HALO:CONV_HALO + tm, :] = ca * jax.nn.sigmoid(cb)
    g0 = c0 + 2 * D_CONV
    ngate = gate_ref.shape[1]

    def after(x):
        x = x[0:1, 0:LANES]
        return jnp.where(x != x, x, 0.0)

    def q_tile(lo):
        r = seg(lo, lo + MXU_COLS)
        q_ref[:, lo:lo + MXU_COLS] = (r * (HEAD_DIM ** -0.5 * LOG2E)).astype(BF16)
        return r

    def kv_tile(ref, base, lo):
        r = seg(base + lo, base + lo + MXU_COLS)
        ref[:, lo:lo + MXU_COLS] = r.astype(BF16)
        return r

    def gate_tile(lo):
        r = seg(g0 + lo, g0 + lo + MXU_COLS)
        gate_ref[:, lo:lo + MXU_COLS] = jax.nn.sigmoid(
            r + gateb_ref[:, lo:lo + MXU_COLS]).astype(BF16)
        return r

    tiles = ([functools.partial(q_tile, lo) for lo in range(0, a, MXU_COLS)]
             + [functools.partial(kv_tile, k_ref, a, lo) for lo in range(0, a, MXU_COLS)]
             + [functools.partial(kv_tile, v_ref, 2 * a, lo) for lo in range(0, a, MXU_COLS)]
             + [functools.partial(gate_tile, lo) for lo in range(0, ngate, MXU_COLS)])

    base = CONV_HALO - (CONV_WIDTH - 1)
    span = CONV_HALO + tm - SUBLANES
    n_blocks = (D_CONV // LANES) * (tm // CONV_ROWS)
    anchor = 0.0
    done = []
    anchored = 0
    blk = 0
    for l0 in range(0, D_CONV, LANES):
        cs = slice(l0, l0 + LANES)
        for sh in range(1, SUBLANES):
            shift_ref[sh, 0:span, :] = cbuf_ref[pl.ds(sh, span), cs]
        for r0 in range(0, tm, CONV_ROWS):
            while len(done) * n_blocks < (blk + 1) * len(tiles):
                done.append(tiles[len(done)]())
            while anchored < len(done) - CONV_TILE_LAG:
                anchor = anchor + after(done[anchored])
                anchored += 1
            acc = jnp.zeros((CONV_ROWS, LANES), F32) + dwb_ref[:, cs]
            for j in range(CONV_WIDTH):
                al, sh = divmod(base + j, SUBLANES)
                if sh == 0:
                    tap = cbuf_ref[pl.ds(r0 + al * SUBLANES, CONV_ROWS), cs]
                else:
                    tap = shift_ref[sh, pl.ds(r0 + al * SUBLANES, CONV_ROWS), :]
                acc = acc + tap * (dww_ref[j:j + 1, cs] + anchor)
            conv_ref[r0:r0 + CONV_ROWS, cs] = acc
            anchor = after(acc)
            blk += 1
    for tile in tiles[len(done):]:
        tile()
    cbuf_ref[0:CONV_HALO, :] = cbuf_ref[tm:tm + CONV_HALO, :]
    cv = conv_ref[...]
    mu = jnp.mean(cv, axis=-1, keepdims=True)
    xc = cv - mu
    var = jnp.mean(xc * xc, axis=-1, keepdims=True)
    ln = xc * lax.rsqrt(var + EPS) * lng_ref[...] + lnb_ref[...]
    act_ref[...] = (ln * jax.nn.sigmoid(ln)).astype(BF16)


def _proj(h, g, win, glub, gateb, dww, dwb, lng, lnb, seq_len):
    t, d = h.shape
    tm = ROW_TILE
    n = t // tm
    din = win.shape[1]
    ngate = gateb.shape[1]
    outs = [jax.ShapeDtypeStruct((t, D_ATTN), BF16)] * 3 + [
        jax.ShapeDtypeStruct((t, D_CONV), BF16), jax.ShapeDtypeStruct((t, ngate), BF16)]
    return pl.pallas_call(
        functools.partial(_proj_kernel, seq_len // tm),
        grid=(n,),
        in_specs=[
            _row_spec(tm, d),
            _const_spec((1, d)),
            _const_spec((d, din)),
            _const_spec((1, 2 * D_CONV)),
            _const_spec((1, ngate)),
            _const_spec(dww.shape),
            _const_spec(dwb.shape),
            _const_spec(lng.shape),
            _const_spec(lnb.shape),
        ],
        out_specs=[_row_spec(tm, D_ATTN)] * 3 + [
            _row_spec(tm, D_CONV), _row_spec(tm, ngate)],
        out_shape=outs,
        scratch_shapes=[
            pltpu.VMEM((d, din), BF16),
            pltpu.VMEM((CONV_HALO + tm, D_CONV), F32),
            pltpu.VMEM((SUBLANES, CONV_HALO + tm, LANES), F32),
            pltpu.VMEM((tm, D_CONV), F32),
        ],
        compiler_params=_params(),
        name="proj",
    )(h, g, win, glub, gateb, dww, dwb, lng, lnb)


def _mixer_kernel(tiles_per_seq,
                  q_ref, kp_ref, kc_ref, vp_ref, vc_ref, act_ref, gate_ref, h_ref,
                  brow_ref, wa32_ref, wc32_ref, wo32_ref, gpost_ref,
                  o_ref, bias_ref, s_ref, attn_ref, wa_ref, wc_ref, wo_ref):
    tm = q_ref.shape[0]
    d = h_ref.shape[1]
    first = (pl.program_id(0) % tiles_per_seq) == 0

    @pl.when(pl.program_id(0) == 0)
    def _():
        wa_ref[...] = wa32_ref[...].astype(BF16)
        wc_ref[...] = wc32_ref[...].astype(BF16)
        wo_ref[...] = wo32_ref[...].astype(BF16)
        r = lax.broadcasted_iota(jnp.int32, (Q_BLOCK, K_BLOCK), 0)
        k = lax.broadcasted_iota(jnp.int32, (Q_BLOCK, K_BLOCK), 1)
        rel = k - r + (r & (CHUNK - 1))
        in_band = jnp.logical_and(rel >= 0, rel < (LEFT_CHUNKS + 1) * CHUNK)
        for hd in range(N_HEADS):
            rows = jnp.broadcast_to(brow_ref[hd:hd + 1, :], (Q_BLOCK, BIAS_ROW_LEN))
            toep = pltpu.roll(rows, 0, 1, stride=1, stride_axis=0)
            bias_ref[hd] = jnp.where(in_band, toep[:, Q_BLOCK:], NEG_INF).T

    lane = lax.broadcasted_iota(jnp.int32, (1, LANES), 1)
    lo_half = lane < HEAD_DIM
    krow = lax.broadcasted_iota(jnp.int32, (2 * tm, LANES), 0)
    klane = lax.broadcasted_iota(jnp.int32, (2 * tm, LANES), 1)
    kneg = jnp.where(jnp.logical_and(first, krow < tm), NEG_INF, 0.0)
    n_pairs = N_HEADS // 2
    per_pair = 2 * (tm // Q_BLOCK)

    def scores(hp):
        sl = slice(LANES * hp, LANES * (hp + 1))
        k2 = jnp.concatenate([kp_ref[:, sl], kc_ref[:, sl]], axis=0)
        q2 = q_ref[:, sl]
        for hh in range(2):
            mine = lo_half if hh == 0 else jnp.logical_not(lo_half)
            spare = HEAD_DIM if hh == 0 else 0
            other = jnp.where(klane == spare, kneg, 0.0).astype(BF16)
            kz = jnp.where(mine, k2, other)
            qz = jnp.where(mine, q2, jnp.ones_like(q2))
            for bi, r0 in enumerate(range(0, tm, Q_BLOCK)):
                s_ref[(hp % 2) * per_pair + hh * (tm // Q_BLOCK) + bi] = lax.dot_general(
                    kz[r0:r0 + K_BLOCK], qz[r0:r0 + Q_BLOCK],
                    (((1,), (1,)), ((), ())), preferred_element_type=F32)

    scores(0)
    for hp in range(n_pairs):
        if hp + 1 < n_pairs:
            scores(hp + 1)
        sl = slice(LANES * hp, LANES * (hp + 1))
        v2 = jnp.concatenate([vp_ref[:, sl], vc_ref[:, sl]], axis=0)
        for hh in range(2):
            hd = 2 * hp + hh
            for bi, r0 in enumerate(range(0, tm, Q_BLOCK)):
                st = s_ref.at[(hp % 2) * per_pair + hh * (tm // Q_BLOCK) + bi]
                cols = []
                sums = []
                for q0 in range(0, Q_BLOCK, LANES):
                    ks = slice(q0, q0 + SOFTMAX_KEYS)
                    qs = slice(q0, q0 + LANES)
                    sc = st[ks, qs] + bias_ref[hd, ks, qs]
                    m = jnp.max(sc, axis=0, keepdims=True)
                    p = jnp.exp2(sc - m)
                    sums.append(jnp.sum(p, axis=0, keepdims=True))
                    pieces = [p.astype(BF16)]
                    if q0 > 0:
                        pieces.insert(0, jnp.zeros((q0, LANES), BF16))
                    if q0 + SOFTMAX_KEYS < K_BLOCK:
                        pieces.append(jnp.zeros((K_BLOCK - q0 - SOFTMAX_KEYS, LANES), BF16))
                    cols.append(jnp.concatenate(pieces, axis=0))
                pt = jnp.concatenate(cols, axis=1)
                ot = lax.dot_general(v2[r0:r0 + K_BLOCK], pt, (((0,), (0,)), ((), ())),
                                     preferred_element_type=F32)
                ot = ot / jnp.concatenate(sums, axis=1)
                attn_ref[hd * HEAD_DIM:(hd + 1) * HEAD_DIM, r0:r0 + Q_BLOCK] = (
                    ot[hh * HEAD_DIM:(hh + 1) * HEAD_DIM].astype(BF16))
    y_a = lax.dot_general(attn_ref[...], wa_ref[...], (((0,), (0,)), ((), ())),
                          preferred_element_type=F32)
    y_b = _mm(act_ref[...], wc_ref[...])

    merged = (gate_ref[:, :d].astype(F32) * y_a + gate_ref[:, d:].astype(F32) * y_b).astype(BF16)
    out = _mm(merged, wo_ref[...])
    o_ref[...] = h_ref[...] + _rms(out, gpost_ref[...])


def _mixer(q, k, v, act, gates, h, brow, wa, wc, wo, gpost, seq_len):
    t, d = h.shape
    tm = ROW_TILE
    tiles_per_seq = seq_len // tm
    prev = lambda i: (jnp.maximum(i - 1, 0), 0)
    return pl.pallas_call(
        functools.partial(_mixer_kernel, tiles_per_seq),
        grid=(t // tm,),
        in_specs=[
            _row_spec(tm, D_ATTN),
            pl.BlockSpec((tm, D_ATTN), prev),
            _row_spec(tm, D_ATTN),
            pl.BlockSpec((tm, D_ATTN), prev),
            _row_spec(tm, D_ATTN),
            _row_spec(tm, D_CONV),
            _row_spec(tm, 2 * d),
            _row_spec(tm, d),
            _const_spec(brow.shape),
            _const_spec(wa.shape),
            _const_spec(wc.shape),
            _const_spec(wo.shape),
            _const_spec(gpost.shape),
        ],
        out_specs=_row_spec(tm, d),
        out_shape=jax.ShapeDtypeStruct((t, d), F32),
        scratch_shapes=[
            pltpu.VMEM((N_HEADS, K_BLOCK, Q_BLOCK), F32),
            pltpu.VMEM((4 * tm // Q_BLOCK, K_BLOCK, Q_BLOCK), F32),
            pltpu.VMEM((D_ATTN, tm), BF16),
            pltpu.VMEM(wa.shape, BF16),
            pltpu.VMEM(wc.shape, BF16),
            pltpu.VMEM(wo.shape, BF16),
        ],
        compiler_params=_params(),
        name="mixer",
    )(q, k, k, v, v, act, gates, h, brow, wa, wc, wo, gpost)


def _rel_bias_rows(rel_table):
    look_back = LEFT_CHUNKS * CHUNK
    n_far = look_back + Q_BLOCK - REL_CLIP
    n_near = BIAS_ROW_LEN - n_far - (2 * REL_CLIP + 1)
    t = rel_table.astype(F32)
    hd = t.shape[0]
    row = jnp.concatenate([jnp.broadcast_to(t[:, -1:], (hd, n_far)), t[:, ::-1],
                           jnp.broadcast_to(t[:, :1], (hd, n_near))], axis=1)
    return row * LOG2E


def kernel(x, ffn1_norm_pre, ffn1_w_gate, ffn1_w_up, ffn1_w_down, ffn1_norm_post, mix_norm_pre, w_in, gate_bias, rel_table, w_attn_out, conv_glu_bias, conv_dw_w, conv_dw_b, conv_ln_g, conv_ln_b, conv_w_out, w_out, mix_norm_post, ffn2_norm_pre, ffn2_w_gate, ffn2_w_up, ffn2_w_down, ffn2_norm_post):
    b, s, d = x.shape
    assert s % ROW_TILE == 0 and d % LANES == 0
    h = x.reshape(b * s, d)
    depth = w_in.shape[0]
    row = lambda p: p.reshape(1, -1)
    for l in range(depth):
        h = _ffn(h, row(ffn1_norm_pre[l]), ffn1_w_gate[l].astype(BF16), ffn1_w_up[l].astype(BF16),
                 ffn1_w_down[l].astype(BF16), row(ffn1_norm_post[l]))
        q, k, v, act, gates = _proj(h, row(mix_norm_pre[l]), w_in[l],
                                    row(conv_glu_bias[l]), row(gate_bias[l]),
                                    conv_dw_w[l].reshape(CONV_WIDTH, D_CONV), row(conv_dw_b[l]),
                                    row(conv_ln_g[l]), row(conv_ln_b[l]), s)
        h = _mixer(q, k, v, act, gates, h, _rel_bias_rows(rel_table[l]), w_attn_out[l],
                   conv_w_out[l], w_out[l], row(mix_norm_post[l]), s)
        h = _ffn(h, row(ffn2_norm_pre[l]), ffn2_w_gate[l].astype(BF16), ffn2_w_up[l].astype(BF16),
                 ffn2_w_down[l].astype(BF16), row(ffn2_norm_post[l]))
    return h.reshape(b, s, d)
```

```python
import functools

import jax
import jax.numpy as jnp
from jax import lax
from jax.experimental import pallas as pl
from jax.experimental.pallas import tpu as pltpu

CHUNK = 64
LEFT_CHUNKS = 8
N_HEADS = 8
HEAD_DIM = 64
D_ATTN = N_HEADS * HEAD_DIM
REL_CLIP = 128
D_CONV = 512
CONV_WIDTH = 31
EPS = 1e-6
NEG_INF = -1e30
LOG2E = 1.4426950408889634

LANES = 128
SUBLANES = 8
MXU_COLS = 256
VMEM_LIMIT_BYTES = 56 * 1024 * 1024

ROW_TILE = LEFT_CHUNKS * CHUNK
FFN_ROWS = 2 * ROW_TILE
FFN_PART = ROW_TILE // 2
Q_BLOCK = 4 * CHUNK
K_BLOCK = Q_BLOCK + LEFT_CHUNKS * CHUNK
SOFTMAX_KEYS = (LEFT_CHUNKS + 1) * CHUNK + (LANES - CHUNK)
BIAS_ROW_LEN = Q_BLOCK + K_BLOCK
CONV_HALO = 32
CONV_ROWS = 64
CONV_TILE_LAG = 3

BF16 = jnp.bfloat16
F32 = jnp.float32


def _rms(x, g):
    ms = jnp.mean(x * x, axis=-1, keepdims=True)
    return x * lax.rsqrt(ms + EPS) * g


def _mm(a, w):
    return lax.dot_general(a, w, (((1,), (0,)), ((), ())), preferred_element_type=F32)


def _params():
    return pltpu.CompilerParams(
        dimension_semantics=("arbitrary",), vmem_limit_bytes=VMEM_LIMIT_BYTES)


def _const_spec(shape):
    nd = len(shape)
    return pl.BlockSpec(shape, lambda i: (0,) * nd, pipeline_mode=pl.Buffered(1))


def _row_spec(rows, cols):
    return pl.BlockSpec((rows, cols), lambda i: (i, 0))


def _ffn_kernel(x_ref, gpre_ref, wg_ref, wu_ref, wd_ref, gpost_ref, o_ref):
    halves = [slice(r0, r0 + FFN_PART) for r0 in range(0, FFN_ROWS, FFN_PART)]
    xn = [_rms(x_ref[rows, :], gpre_ref[...]).astype(BF16) for rows in halves]
    act = []
    for xh in xn:
        g = _mm(xh, wg_ref[...])
        u = _mm(xh, wu_ref[...])
        act.append((g * jax.nn.sigmoid(g) * u).astype(BF16))
    f = [_mm(a, wd_ref[...]) for a in act]
    for rows, fh in zip(halves, f):
        o_ref[rows, :] = x_ref[rows, :] + 0.5 * _rms(fh, gpost_ref[...])


def _ffn(x, gpre, wg, wu, wd, gpost):
    t, d = x.shape
    dff = wg.shape[1]
    return pl.pallas_call(
        _ffn_kernel,
        grid=(t // FFN_ROWS,),
        in_specs=[
            _row_spec(FFN_ROWS, d),
            _const_spec((1, d)),
            _const_spec((d, dff)),
            _const_spec((d, dff)),
            _const_spec((dff, d)),
            _const_spec((1, d)),
        ],
        out_specs=_row_spec(FFN_ROWS, d),
        out_shape=jax.ShapeDtypeStruct((t, d), F32),
        compiler_params=_params(),
        name="ffn",
    )(x, gpre, wg, wu, wd, gpost)


def _proj_kernel(tiles_per_seq,
                 h_ref, g_ref, win_ref, glub_ref, gateb_ref, dww_ref, dwb_ref, lng_ref, lnb_ref,
                 q_ref, k_ref, v_ref, act_ref, gate_ref, w_ref, cbuf_ref, shift_ref, conv_ref):
    tm = h_ref.shape[0]

    @pl.when(pl.program_id(0) == 0)
    def _():
        w_ref[...] = win_ref[...].astype(BF16)

    u = _rms(h_ref[...], g_ref[...]).astype(BF16)

    def seg(lo, hi):
        return _mm(u, w_ref[:, lo:hi])

    @pl.when((pl.program_id(0) % tiles_per_seq) == 0)
    def _():
        cbuf_ref[0:CONV_HALO, :] = jnp.zeros((CONV_HALO, D_CONV), F32)

    a = D_ATTN
    c0 = 3 * a
    ca = seg(c0, c0 + D_CONV) + glub_ref[:, :D_CONV]
    cb = seg(c0 + D_CONV, c0 + 2 * D_CONV) + glub_ref[:, D_CONV:]
    cbuf_ref[CONV_HALO:CONV_HALO + tm, :] = ca * jax.nn.sigmoid(cb)
    g0 = c0 + 2 * D_CONV
    ngate = gate_ref.shape[1]

    def after(x):
        x = x[0:1, 0:LANES]
        return jnp.where(x != x, x, 0.0)

    def q_tile(lo):
        r = seg(lo, lo + MXU_COLS)
        q_ref[:, lo:lo + MXU_COLS] = (r * (HEAD_DIM ** -0.5 * LOG2E)).astype(BF16)
        return r

    def kv_tile(ref, base, lo):
        r = seg(base + lo, base + lo + MXU_COLS)
        ref[:, lo:lo + MXU_COLS] = r.astype(BF16)
        return r

    def gate_tile(lo):
        r = seg(g0 + lo, g0 + lo + MXU_COLS)
        gate_ref[:, lo:lo + MXU_COLS] = (r + gateb_ref[:, lo:lo + MXU_COLS]).astype(BF16)
        return r

    tiles = ([functools.partial(q_tile, lo) for lo in range(0, a, MXU_COLS)]
             + [functools.partial(kv_tile, k_ref, a, lo) for lo in range(0, a, MXU_COLS)]
             + [functools.partial(kv_tile, v_ref, 2 * a, lo) for lo in range(0, a, MXU_COLS)]
             + [functools.partial(gate_tile, lo) for lo in range(0, ngate, MXU_COLS)])

    base = CONV_HALO - (CONV_WIDTH - 1)
    span = CONV_HALO + tm - SUBLANES
    n_blocks = (D_CONV // LANES) * (tm // CONV_ROWS)
    anchor = 0.0
    done = []
    anchored = 0
    blk = 0
    for l0 in range(0, D_CONV, LANES):
        cs = slice(l0, l0 + LANES)
        for sh in range(1, SUBLANES):
            shift_ref[sh, 0:span, :] = cbuf_ref[pl.ds(sh, span), cs]
        for r0 in range(0, tm, CONV_ROWS):
            while len(done) * n_blocks < (blk + 1) * len(tiles):
                done.append(tiles[len(done)]())
            while anchored < len(done) - CONV_TILE_LAG:
                anchor = anchor + after(done[anchored])
                anchored += 1
            acc = jnp.zeros((CONV_ROWS, LANES), F32) + dwb_ref[:, cs]
            for j in range(CONV_WIDTH):
                al, sh = divmod(base + j, SUBLANES)
                if sh == 0:
                    tap = cbuf_ref[pl.ds(r0 + al * SUBLANES, CONV_ROWS), cs]
                else:
                    tap = shift_ref[sh, pl.ds(r0 + al * SUBLANES, CONV_ROWS), :]
                acc = acc + tap * (dww_ref[j:j + 1, cs] + anchor)
            conv_ref[r0:r0 + CONV_ROWS, cs] = acc
            anchor = after(acc)
            blk += 1
    for tile in tiles[len(done):]:
        tile()
    cbuf_ref[0:CONV_HALO, :] = cbuf_ref[tm:tm + CONV_HALO, :]
    cv = conv_ref[...]
    mu = jnp.mean(cv, axis=-1, keepdims=True)
    xc = cv - mu
    var = jnp.mean(xc * xc, axis=-1, keepdims=True)
    ln = xc * lax.rsqrt(var + EPS) * lng_ref[...] + lnb_ref[...]
    act_ref[...] = (ln * jax.nn.sigmoid(ln)).astype(BF16)


def _proj(h, g, win, glub, gateb, dww, dwb, lng, lnb, seq_len):
    t, d = h.shape
    tm = ROW_TILE
    n = t // tm
    din = win.shape[1]
    ngate = gateb.shape[1]
    outs = [jax.ShapeDtypeStruct((t, D_ATTN), BF16)] * 3 + [
        jax.ShapeDtypeStruct((t, D_CONV), BF16), jax.ShapeDtypeStruct((t, ngate), BF16)]
    return pl.pallas_call(
        functools.partial(_proj_kernel, seq_len // tm),
        grid=(n,),
        in_specs=[
            _row_spec(tm, d),
            _const_spec((1, d)),
            _const_spec((d, din)),
            _const_spec((1, 2 * D_CONV)),
            _const_spec((1, ngate)),
            _const_spec(dww.shape),
            _const_spec(dwb.shape),
            _const_spec(lng.shape),
            _const_spec(lnb.shape),
        ],
        out_specs=[_row_spec(tm, D_ATTN)] * 3 + [
            _row_spec(tm, D_CONV), _row_spec(tm, ngate)],
        out_shape=outs,
        scratch_shapes=[
            pltpu.VMEM((d, din), BF16),
            pltpu.VMEM((CONV_HALO + tm, D_CONV), F32),
            pltpu.VMEM((SUBLANES, CONV_HALO + tm, LANES), F32),
            pltpu.VMEM((tm, D_CONV), F32),
        ],
        compiler_params=_params(),
        name="proj",
    )(h, g, win, glub, gateb, dww, dwb, lng, lnb)


def _mixer_kernel(tiles_per_seq,
                  q_ref, kp_ref, kc_ref, vp_ref, vc_ref, act_ref, gate_ref, h_ref,
                  brow_ref, wa32_ref, wc32_ref, wo32_ref, gpost_ref,
                  o_ref, bias_ref, s_ref, attn_ref, wa_ref, wc_ref, wo_ref):
    tm = q_ref.shape[0]
    d = h_ref.shape[1]
    first = (pl.program_id(0) % tiles_per_seq) == 0

    @pl.when(pl.program_id(0) == 0)
    def _():
        wa_ref[...] = wa32_ref[...].astype(BF16)
        wc_ref[...] = wc32_ref[...].astype(BF16)
        wo_ref[...] = wo32_ref[...].astype(BF16)
        r = lax.broadcasted_iota(jnp.int32, (Q_BLOCK, K_BLOCK), 0)
        k = lax.broadcasted_iota(jnp.int32, (Q_BLOCK, K_BLOCK), 1)
        rel = k - r + (r & (CHUNK - 1))
        in_band = jnp.logical_and(rel >= 0, rel < (LEFT_CHUNKS + 1) * CHUNK)
        for hd in range(N_HEADS):
            rows = jnp.broadcast_to(brow_ref[hd:hd + 1, :], (Q_BLOCK, BIAS_ROW_LEN))
            toep = pltpu.roll(rows, 0, 1, stride=1, stride_axis=0)
            bias_ref[hd] = jnp.where(in_band, toep[:, Q_BLOCK:], NEG_INF).T

    lane = lax.broadcasted_iota(jnp.int32, (1, LANES), 1)
    lo_half = lane < HEAD_DIM
    krow = lax.broadcasted_iota(jnp.int32, (2 * tm, LANES), 0)
    klane = lax.broadcasted_iota(jnp.int32, (2 * tm, LANES), 1)
    kneg = jnp.where(jnp.logical_and(first, krow < tm), NEG_INF, 0.0)
    n_pairs = N_HEADS // 2
    per_pair = 2 * (tm // Q_BLOCK)

    def scores(hp):
        sl = slice(LANES * hp, LANES * (hp + 1))
        k2 = jnp.concatenate([kp_ref[:, sl], kc_ref[:, sl]], axis=0)
        q2 = q_ref[:, sl]
        for hh in range(2):
            mine = lo_half if hh == 0 else jnp.logical_not(lo_half)
            spare = HEAD_DIM if hh == 0 else 0
            other = jnp.where(klane == spare, kneg, 0.0).astype(BF16)
            kz = jnp.where(mine, k2, other)
            qz = jnp.where(mine, q2, jnp.ones_like(q2))
            for bi, r0 in enumerate(range(0, tm, Q_BLOCK)):
                s_ref[(hp % 2) * per_pair + hh * (tm // Q_BLOCK) + bi] = lax.dot_general(
                    kz[r0:r0 + K_BLOCK], qz[r0:r0 + Q_BLOCK],
                    (((1,), (1,)), ((), ())), preferred_element_type=F32)

    scores(0)
    for hp in range(n_pairs):
        if hp + 1 < n_pairs:
            scores(hp + 1)
        sl = slice(LANES * hp, LANES * (hp + 1))
        v2 = jnp.concatenate([vp_ref[:, sl], vc_ref[:, sl]], axis=0)
        for hh in range(2):
            hd = 2 * hp + hh
            for bi, r0 in enumerate(range(0, tm, Q_BLOCK)):
                st = s_ref.at[(hp % 2) * per_pair + hh * (tm // Q_BLOCK) + bi]
                cols = []
                sums = []
                for q0 in range(0, Q_BLOCK, LANES):
                    ks = slice(q0, q0 + SOFTMAX_KEYS)
                    qs = slice(q0, q0 + LANES)
                    sc = st[ks, qs] + bias_ref[hd, ks, qs]
                    m = jnp.max(sc, axis=0, keepdims=True)
                    p = jnp.exp2(sc - m)
                    sums.append(jnp.sum(p, axis=0, keepdims=True))
                    pieces = [p.astype(BF16)]
                    if q0 > 0:
                        pieces.insert(0, jnp.zeros((q0, LANES), BF16))
                    if q0 + SOFTMAX_KEYS < K_BLOCK:
                        pieces.append(jnp.zeros((K_BLOCK - q0 - SOFTMAX_KEYS, LANES), BF16))
                    cols.append(jnp.concatenate(pieces, axis=0))
                pt = jnp.concatenate(cols, axis=1)
                ot = lax.dot_general(v2[r0:r0 + K_BLOCK], pt, (((0,), (0,)), ((), ())),
                                     preferred_element_type=F32)
                ot = ot / jnp.concatenate(sums, axis=1)
                attn_ref[hd * HEAD_DIM:(hd + 1) * HEAD_DIM, r0:r0 + Q_BLOCK] = (
                    ot[hh * HEAD_DIM:(hh + 1) * HEAD_DIM].astype(BF16))
    y_a = lax.dot_general(attn_ref[...], wa_ref[...], (((0,), (0,)), ((), ())),
                          preferred_element_type=F32)
    y_b = _mm(act_ref[...], wc_ref[...])

    gate_a = jax.nn.sigmoid(gate_ref[:, :d].astype(F32))
    gate_b = jax.nn.sigmoid(gate_ref[:, d:].astype(F32))
    merged = (gate_a * y_a + gate_b * y_b).astype(BF16)
    out = _mm(merged, wo_ref[...])
    o_ref[...] = h_ref[...] + _rms(out, gpost_ref[...])


def _mixer(q, k, v, act, gates, h, brow, wa, wc, wo, gpost, seq_len):
    t, d = h.shape
    tm = ROW_TILE
    tiles_per_seq = seq_len // tm
    prev = lambda i: (jnp.maximum(i - 1, 0), 0)
    return pl.pallas_call(
        functools.partial(_mixer_kernel, tiles_per_seq),
        grid=(t // tm,),
        in_specs=[
            _row_spec(tm, D_ATTN),
            pl.BlockSpec((tm, D_ATTN), prev),
            _row_spec(tm, D_ATTN),
            pl.BlockSpec((tm, D_ATTN), prev),
            _row_spec(tm, D_ATTN),
            _row_spec(tm, D_CONV),
            _row_spec(tm, 2 * d),
            _row_spec(tm, d),
            _const_spec(brow.shape),
            _const_spec(wa.shape),
            _const_spec(wc.shape),
            _const_spec(wo.shape),
            _const_spec(gpost.shape),
        ],
        out_specs=_row_spec(tm, d),
        out_shape=jax.ShapeDtypeStruct((t, d), F32),
        scratch_shapes=[
            pltpu.VMEM((N_HEADS, K_BLOCK, Q_BLOCK), F32),
            pltpu.VMEM((4 * tm // Q_BLOCK, K_BLOCK, Q_BLOCK), F32),
            pltpu.VMEM((D_ATTN, tm), BF16),
            pltpu.VMEM(wa.shape, BF16),
            pltpu.VMEM(wc.shape, BF16),
            pltpu.VMEM(wo.shape, BF16),
        ],
        compiler_params=_params(),
        name="mixer",
    )(q, k, k, v, v, act, gates, h, brow, wa, wc, wo, gpost)


def _rel_bias_rows(rel_table):
    look_back = LEFT_CHUNKS * CHUNK
    n_far = look_back + Q_BLOCK - REL_CLIP
    n_near = BIAS_ROW_LEN - n_far - (2 * REL_CLIP + 1)
    t = rel_table.astype(F32)
    hd = t.shape[0]
    row = jnp.concatenate([jnp.broadcast_to(t[:, -1:], (hd, n_far)), t[:, ::-1],
                           jnp.broadcast_to(t[:, :1], (hd, n_near))], axis=1)
    return row * LOG2E


def kernel(x, ffn1_norm_pre, ffn1_w_gate, ffn1_w_up, ffn1_w_down, ffn1_norm_post, mix_norm_pre, w_in, gate_bias, rel_table, w_attn_out, conv_glu_bias, conv_dw_w, conv_dw_b, conv_ln_g, conv_ln_b, conv_w_out, w_out, mix_norm_post, ffn2_norm_pre, ffn2_w_gate, ffn2_w_up, ffn2_w_down, ffn2_norm_post):
    b, s, d = x.shape
    assert s % ROW_TILE == 0 and d % LANES == 0
    h = x.reshape(b * s, d)
    depth = w_in.shape[0]
    row = lambda p: p.reshape(1, -1)
    for l in range(depth):
        h = _ffn(h, row(ffn1_norm_pre[l]), ffn1_w_gate[l].astype(BF16), ffn1_w_up[l].astype(BF16),
                 ffn1_w_down[l].astype(BF16), row(ffn1_norm_post[l]))
        q, k, v, act, gates = _proj(h, row(mix_norm_pre[l]), w_in[l],
                                    row(conv_glu_bias[l]), row(gate_bias[l]),
                                    conv_dw_w[l].reshape(CONV_WIDTH, D_CONV), row(conv_dw_b[l]),
                                    row(conv_ln_g[l]), row(conv_ln_b[l]), s)
        h = _mixer(q, k, v, act, gates, h, _rel_bias_rows(rel_table[l]), w_attn_out[l],
                   conv_w_out[l], w_out[l], row(mix_norm_post[l]), s)
        h = _ffn(h, row(ffn2_norm_pre[l]), ffn2_w_gate[l].astype(BF16), ffn2_w_up[l].astype(BF16),
                 ffn2_w_down[l].astype(BF16), row(ffn2_norm_post[l]))
    return h.reshape(b, s, d)
```

```python
import functools

import jax
import jax.numpy as jnp
from jax import lax
from jax.experimental import pallas as pl
from jax.experimental.pallas import tpu as pltpu

CHUNK = 64
LEFT_CHUNKS = 8
N_HEADS = 8
HEAD_DIM = 64
D_ATTN = N_HEADS * HEAD_DIM
REL_CLIP = 128
D_CONV = 512
CONV_WIDTH = 31
EPS = 1e-6
NEG_INF = -1e30
LOG2E = 1.4426950408889634

LANES = 128
SUBLANES = 8
MXU_COLS = 256
VMEM_LIMIT_BYTES = 56 * 1024 * 1024

ROW_TILE = LEFT_CHUNKS * CHUNK
FFN_ROWS = 2 * ROW_TILE
FFN_PART = ROW_TILE // 2
Q_BLOCK = 4 * CHUNK
K_BLOCK = Q_BLOCK + LEFT_CHUNKS * CHUNK
SOFTMAX_KEYS = (LEFT_CHUNKS + 1) * CHUNK + (LANES - CHUNK)
BIAS_ROW_LEN = Q_BLOCK + K_BLOCK
CONV_HALO = 32
CONV_ROWS = 64
CONV_TILE_LAG = 3

BF16 = jnp.bfloat16
F32 = jnp.float32


def _rms(x, g):
    ms = jnp.mean(x * x, axis=-1, keepdims=True)
    return x * lax.rsqrt(ms + EPS) * g


def _mm(a, w):
    return lax.dot_general(a, w, (((1,), (0,)), ((), ())), preferred_element_type=F32)


def _params():
    return pltpu.CompilerParams(
        dimension_semantics=("arbitrary",), vmem_limit_bytes=VMEM_LIMIT_BYTES)


def _const_spec(shape):
    nd = len(shape)
    return pl.BlockSpec(shape, lambda i: (0,) * nd, pipeline_mode=pl.Buffered(1))


def _row_spec(rows, cols):
    return pl.BlockSpec((rows, cols), lambda i: (i, 0))


def _ffn_kernel(x_ref, gpre_ref, wg_ref, wu_ref, wd_ref, gpost_ref, o_ref):
    halves = [slice(r0, r0 + FFN_PART) for r0 in range(0, FFN_ROWS, FFN_PART)]
    xn = [_rms(x_ref[rows, :], gpre_ref[...]).astype(BF16) for rows in halves]
    act = []
    for xh in xn:
        g = _mm(xh, wg_ref[...])
        u = _mm(xh, wu_ref[...])
        act.append((g * jax.nn.sigmoid(g) * u).astype(BF16))
    f = [_mm(a, wd_ref[...]) for a in act]
    for rows, fh in zip(halves, f):
        o_ref[rows, :] = x_ref[rows, :] + 0.5 * _rms(fh, gpost_ref[...])


def _ffn(x, gpre, wg, wu, wd, gpost):
    t, d = x.shape
    dff = wg.shape[1]
    return pl.pallas_call(
        _ffn_kernel,
        grid=(t // FFN_ROWS,),
        in_specs=[
            _row_spec(FFN_ROWS, d),
            _const_spec((1, d)),
            _const_spec((d, dff)),
            _const_spec((d, dff)),
            _const_spec((dff, d)),
            _const_spec((1, d)),
        ],
        out_specs=_row_spec(FFN_ROWS, d),
        out_shape=jax.ShapeDtypeStruct((t, d), F32),
        compiler_params=_params(),
        name="ffn",
    )(x, gpre, wg, wu, wd, gpost)


def _proj_kernel(tiles_per_seq,
                 h_ref, g_ref, win_ref, glub_ref, gateb_ref, dww_ref, dwb_ref, lng_ref, lnb_ref,
                 kvq_ref, ag_ref, w_ref, cbuf_ref, shift_ref, conv_ref):
    tm = h_ref.shape[0]
    k_ref = kvq_ref.at[:, 0:D_ATTN]
    v_ref = kvq_ref.at[:, D_ATTN:2 * D_ATTN]
    q_ref = kvq_ref.at[:, 2 * D_ATTN:3 * D_ATTN]
    act_ref = ag_ref.at[:, 0:D_CONV]
    gate_ref = ag_ref.at[:, D_CONV:]

    @pl.when(pl.program_id(0) == 0)
    def _():
        w_ref[...] = win_ref[...].astype(BF16)

    u = _rms(h_ref[...], g_ref[...]).astype(BF16)

    def seg(lo, hi):
        return _mm(u, w_ref[:, lo:hi])

    @pl.when((pl.program_id(0) % tiles_per_seq) == 0)
    def _():
        cbuf_ref[0:CONV_HALO, :] = jnp.zeros((CONV_HALO, D_CONV), F32)

    a = D_ATTN
    c0 = 3 * a
    ca = seg(c0, c0 + D_CONV) + glub_ref[:, :D_CONV]
    cb = seg(c0 + D_CONV, c0 + 2 * D_CONV) + glub_ref[:, D_CONV:]
    cbuf_ref[CONV_HALO:CONV_HALO + tm, :] = ca * jax.nn.sigmoid(cb)
    g0 = c0 + 2 * D_CONV
    ngate = gate_ref.shape[1]

    def after(x):
        x = x[0:1, 0:LANES]
        return jnp.where(x != x, x, 0.0)

    def q_tile(lo):
        r = seg(lo, lo + MXU_COLS)
        q_ref[:, lo:lo + MXU_COLS] = (r * (HEAD_DIM ** -0.5 * LOG2E)).astype(BF16)
        return r

    def kv_tile(ref, base, lo):
        r = seg(base + lo, base + lo + MXU_COLS)
        ref[:, lo:lo + MXU_COLS] = r.astype(BF16)
        return r

    def gate_tile(lo):
        r = seg(g0 + lo, g0 + lo + MXU_COLS)
        gate_ref[:, lo:lo + MXU_COLS] = (r + gateb_ref[:, lo:lo + MXU_COLS]).astype(BF16)
        return r

    tiles = ([functools.partial(q_tile, lo) for lo in range(0, a, MXU_COLS)]
             + [functools.partial(kv_tile, k_ref, a, lo) for lo in range(0, a, MXU_COLS)]
             + [functools.partial(kv_tile, v_ref, 2 * a, lo) for lo in range(0, a, MXU_COLS)]
             + [functools.partial(gate_tile, lo) for lo in range(0, ngate, MXU_COLS)])

    base = CONV_HALO - (CONV_WIDTH - 1)
    span = CONV_HALO + tm - SUBLANES
    n_blocks = (D_CONV // LANES) * (tm // CONV_ROWS)
    anchor = 0.0
    done = []
    anchored = 0
    blk = 0
    for l0 in range(0, D_CONV, LANES):
        cs = slice(l0, l0 + LANES)
        for sh in range(1, SUBLANES):
            shift_ref[sh, 0:span, :] = cbuf_ref[pl.ds(sh, span), cs]
        for r0 in range(0, tm, CONV_ROWS):
            while len(done) * n_blocks < (blk + 1) * len(tiles):
                done.append(tiles[len(done)]())
            while anchored < len(done) - CONV_TILE_LAG:
                anchor = anchor + after(done[anchored])
                anchored += 1
            acc = jnp.zeros((CONV_ROWS, LANES), F32) + dwb_ref[:, cs]
            for j in range(CONV_WIDTH):
                al, sh = divmod(base + j, SUBLANES)
                if sh == 0:
                    tap = cbuf_ref[pl.ds(r0 + al * SUBLANES, CONV_ROWS), cs]
                else:
                    tap = shift_ref[sh, pl.ds(r0 + al * SUBLANES, CONV_ROWS), :]
                acc = acc + tap * (dww_ref[j:j + 1, cs] + anchor)
            conv_ref[r0:r0 + CONV_ROWS, cs] = acc
            anchor = after(acc)
            blk += 1
    for tile in tiles[len(done):]:
        tile()
    cbuf_ref[0:CONV_HALO, :] = cbuf_ref[tm:tm + CONV_HALO, :]
    cv = conv_ref[...]
    mu = jnp.mean(cv, axis=-1, keepdims=True)
    xc = cv - mu
    var = jnp.mean(xc * xc, axis=-1, keepdims=True)
    ln = xc * lax.rsqrt(var + EPS) * lng_ref[...] + lnb_ref[...]
    act_ref[...] = (ln * jax.nn.sigmoid(ln)).astype(BF16)


def _proj(h, g, win, glub, gateb, dww, dwb, lng, lnb, seq_len):
    t, d = h.shape
    tm = ROW_TILE
    n = t // tm
    din = win.shape[1]
    ngate = gateb.shape[1]
    outs = [jax.ShapeDtypeStruct((t, 3 * D_ATTN), BF16),
            jax.ShapeDtypeStruct((t, D_CONV + ngate), BF16)]
    return pl.pallas_call(
        functools.partial(_proj_kernel, seq_len // tm),
        grid=(n,),
        in_specs=[
            _row_spec(tm, d),
            _const_spec((1, d)),
            _const_spec((d, din)),
            _const_spec((1, 2 * D_CONV)),
            _const_spec((1, ngate)),
            _const_spec(dww.shape),
            _const_spec(dwb.shape),
            _const_spec(lng.shape),
            _const_spec(lnb.shape),
        ],
        out_specs=[_row_spec(tm, 3 * D_ATTN), _row_spec(tm, D_CONV + ngate)],
        out_shape=outs,
        scratch_shapes=[
            pltpu.VMEM((d, din), BF16),
            pltpu.VMEM((CONV_HALO + tm, D_CONV), F32),
            pltpu.VMEM((SUBLANES, CONV_HALO + tm, LANES), F32),
            pltpu.VMEM((tm, D_CONV), F32),
        ],
        compiler_params=_params(),
        name="proj",
    )(h, g, win, glub, gateb, dww, dwb, lng, lnb)


def _mixer_kernel(tiles_per_seq,
                  kvq_ref, kvp_ref, ag_ref, h_ref,
                  brow_ref, wa32_ref, wc32_ref, wo32_ref, gpost_ref,
                  o_ref, bias_ref, s_ref, attn_ref, wa_ref, wc_ref, wo_ref):
    tm, d = h_ref.shape
    kc_ref = kvq_ref.at[:, 0:D_ATTN]
    vc_ref = kvq_ref.at[:, D_ATTN:2 * D_ATTN]
    q_ref = kvq_ref.at[:, 2 * D_ATTN:3 * D_ATTN]
    kp_ref = kvp_ref.at[:, 0:D_ATTN]
    vp_ref = kvp_ref.at[:, D_ATTN:2 * D_ATTN]
    act_ref = ag_ref.at[:, 0:D_CONV]
    gate_ref = ag_ref.at[:, D_CONV:]
    first = (pl.program_id(0) % tiles_per_seq) == 0

    @pl.when(pl.program_id(0) == 0)
    def _():
        wa_ref[...] = wa32_ref[...].astype(BF16)
        wc_ref[...] = wc32_ref[...].astype(BF16)
        wo_ref[...] = wo32_ref[...].astype(BF16)
        r = lax.broadcasted_iota(jnp.int32, (Q_BLOCK, K_BLOCK), 0)
        k = lax.broadcasted_iota(jnp.int32, (Q_BLOCK, K_BLOCK), 1)
        rel = k - r + (r & (CHUNK - 1))
        in_band = jnp.logical_and(rel >= 0, rel < (LEFT_CHUNKS + 1) * CHUNK)
        for hd in range(N_HEADS):
            rows = jnp.broadcast_to(brow_ref[hd:hd + 1, :], (Q_BLOCK, BIAS_ROW_LEN))
            toep = pltpu.roll(rows, 0, 1, stride=1, stride_axis=0)
            bias_ref[hd] = jnp.where(in_band, toep[:, Q_BLOCK:], NEG_INF).T

    lane = lax.broadcasted_iota(jnp.int32, (1, LANES), 1)
    lo_half = lane < HEAD_DIM
    krow = lax.broadcasted_iota(jnp.int32, (2 * tm, LANES), 0)
    klane = lax.broadcasted_iota(jnp.int32, (2 * tm, LANES), 1)
    kneg = jnp.where(jnp.logical_and(first, krow < tm), NEG_INF, 0.0)
    n_pairs = N_HEADS // 2
    per_pair = 2 * (tm // Q_BLOCK)

    def scores(hp):
        sl = slice(LANES * hp, LANES * (hp + 1))
        k2 = jnp.concatenate([kp_ref[:, sl], kc_ref[:, sl]], axis=0)
        q2 = q_ref[:, sl]
        for hh in range(2):
            mine = lo_half if hh == 0 else jnp.logical_not(lo_half)
            spare = HEAD_DIM if hh == 0 else 0
            other = jnp.where(klane == spare, kneg, 0.0).astype(BF16)
            kz = jnp.where(mine, k2, other)
            qz = jnp.where(mine, q2, jnp.ones_like(q2))
            for bi, r0 in enumerate(range(0, tm, Q_BLOCK)):
                s_ref[(hp % 2) * per_pair + hh * (tm // Q_BLOCK) + bi] = lax.dot_general(
                    kz[r0:r0 + K_BLOCK], qz[r0:r0 + Q_BLOCK],
                    (((1,), (1,)), ((), ())), preferred_element_type=F32)

    scores(0)
    for hp in range(n_pairs):
        if hp + 1 < n_pairs:
            scores(hp + 1)
        sl = slice(LANES * hp, LANES * (hp + 1))
        v2 = jnp.concatenate([vp_ref[:, sl], vc_ref[:, sl]], axis=0)
        for hh in range(2):
            hd = 2 * hp + hh
            for bi, r0 in enumerate(range(0, tm, Q_BLOCK)):
                st = s_ref.at[(hp % 2) * per_pair + hh * (tm // Q_BLOCK) + bi]
                cols = []
                sums = []
                for q0 in range(0, Q_BLOCK, LANES):
                    ks = slice(q0, q0 + SOFTMAX_KEYS)
                    qs = slice(q0, q0 + LANES)
                    sc = st[ks, qs] + bias_ref[hd, ks, qs]
                    m = jnp.max(sc, axis=0, keepdims=True)
                    p = jnp.exp2(sc - m)
                    sums.append(jnp.sum(p, axis=0, keepdims=True))
                    pieces = [p.astype(BF16)]
                    if q0 > 0:
                        pieces.insert(0, jnp.zeros((q0, LANES), BF16))
                    if q0 + SOFTMAX_KEYS < K_BLOCK:
                        pieces.append(jnp.zeros((K_BLOCK - q0 - SOFTMAX_KEYS, LANES), BF16))
                    cols.append(jnp.concatenate(pieces, axis=0))
                pt = jnp.concatenate(cols, axis=1)
                ot = lax.dot_general(v2[r0:r0 + K_BLOCK], pt, (((0,), (0,)), ((), ())),
                                     preferred_element_type=F32)
                ot = ot / jnp.concatenate(sums, axis=1)
                attn_ref[hd * HEAD_DIM:(hd + 1) * HEAD_DIM, r0:r0 + Q_BLOCK] = (
                    ot[hh * HEAD_DIM:(hh + 1) * HEAD_DIM].astype(BF16))
    y_a = lax.dot_general(attn_ref[...], wa_ref[...], (((0,), (0,)), ((), ())),
                          preferred_element_type=F32)
    y_b = _mm(act_ref[...], wc_ref[...])

    gate_a = jax.nn.sigmoid(gate_ref[:, :d].astype(F32))
    gate_b = jax.nn.sigmoid(gate_ref[:, d:].astype(F32))
    merged = (gate_a * y_a + gate_b * y_b).astype(BF16)
    out = _mm(merged, wo_ref[...])
    o_ref[...] = h_ref[...] + _rms(out, gpost_ref[...])


def _mixer(kvq, act_gates, h, brow, wa, wc, wo, gpost, seq_len):
    t, d = h.shape
    tm = ROW_TILE
    tiles_per_seq = seq_len // tm
    prev = lambda i: (jnp.maximum(i - 1, 0), 0)
    return pl.pallas_call(
        functools.partial(_mixer_kernel, tiles_per_seq),
        grid=(t // tm,),
        in_specs=[
            _row_spec(tm, 3 * D_ATTN),
            pl.BlockSpec((tm, 2 * D_ATTN), prev),
            _row_spec(tm, act_gates.shape[1]),
            _row_spec(tm, d),
            _const_spec(brow.shape),
            _const_spec(wa.shape),
            _const_spec(wc.shape),
            _const_spec(wo.shape),
            _const_spec(gpost.shape),
        ],
        out_specs=_row_spec(tm, d),
        out_shape=jax.ShapeDtypeStruct((t, d), F32),
        scratch_shapes=[
            pltpu.VMEM((N_HEADS, K_BLOCK, Q_BLOCK), F32),
            pltpu.VMEM((4 * tm // Q_BLOCK, K_BLOCK, Q_BLOCK), F32),
            pltpu.VMEM((D_ATTN, tm), BF16),
            pltpu.VMEM(wa.shape, BF16),
            pltpu.VMEM(wc.shape, BF16),
            pltpu.VMEM(wo.shape, BF16),
        ],
        compiler_params=_params(),
        name="mixer",
    )(kvq, kvq, act_gates, h, brow, wa, wc, wo, gpost)


def _rel_bias_rows(rel_table):
    look_back = LEFT_CHUNKS * CHUNK
    n_far = look_back + Q_BLOCK - REL_CLIP
    n_near = BIAS_ROW_LEN - n_far - (2 * REL_CLIP + 1)
    t = rel_table.astype(F32)
    hd = t.shape[0]
    row = jnp.concatenate([jnp.broadcast_to(t[:, -1:], (hd, n_far)), t[:, ::-1],
                           jnp.broadcast_to(t[:, :1], (hd, n_near))], axis=1)
    return row * LOG2E


def kernel(x, ffn1_norm_pre, ffn1_w_gate, ffn1_w_up, ffn1_w_down, ffn1_norm_post, mix_norm_pre, w_in, gate_bias, rel_table, w_attn_out, conv_glu_bias, conv_dw_w, conv_dw_b, conv_ln_g, conv_ln_b, conv_w_out, w_out, mix_norm_post, ffn2_norm_pre, ffn2_w_gate, ffn2_w_up, ffn2_w_down, ffn2_norm_post):
    b, s, d = x.shape
    assert s % ROW_TILE == 0 and d % LANES == 0
    h = x.reshape(b * s, d)
    depth = w_in.shape[0]
    row = lambda p: p.reshape(1, -1)
    for l in range(depth):
        h = _ffn(h, row(ffn1_norm_pre[l]), ffn1_w_gate[l].astype(BF16), ffn1_w_up[l].astype(BF16),
                 ffn1_w_down[l].astype(BF16), row(ffn1_norm_post[l]))
        kvq, act_gates = _proj(h, row(mix_norm_pre[l]), w_in[l],
                               row(conv_glu_bias[l]), row(gate_bias[l]),
                               conv_dw_w[l].reshape(CONV_WIDTH, D_CONV), row(conv_dw_b[l]),
                               row(conv_ln_g[l]), row(conv_ln_b[l]), s)
        h = _mixer(kvq, act_gates, h, _rel_bias_rows(rel_table[l]), w_attn_out[l],
                   conv_w_out[l], w_out[l], row(mix_norm_post[l]), s)
        h = _ffn(h, row(ffn2_norm_pre[l]), ffn2_w_gate[l].astype(BF16), ffn2_w_up[l].astype(BF16),
                 ffn2_w_down[l].astype(BF16), row(ffn2_norm_post[l]))
    return h.reshape(b, s, d)
```

```python
import functools

import jax
import jax.numpy as jnp
from jax import lax
from jax.experimental import pallas as pl
from jax.experimental.pallas import tpu as pltpu

CHUNK = 64
LEFT_CHUNKS = 8
N_HEADS = 8
HEAD_DIM = 64
D_ATTN = N_HEADS * HEAD_DIM
REL_CLIP = 128
D_CONV = 512
CONV_WIDTH = 31
EPS = 1e-6
NEG_INF = -1e30
LOG2E = 1.4426950408889634

LANES = 128
SUBLANES = 8
MXU_COLS = 256
VMEM_LIMIT_BYTES = 56 * 1024 * 1024

ROW_TILE = LEFT_CHUNKS * CHUNK
FFN_ROWS = 2 * ROW_TILE
FFN_PART = ROW_TILE // 2
FFN_STAGE_CHUNKS = 8
Q_BLOCK = 4 * CHUNK
K_BLOCK = Q_BLOCK + LEFT_CHUNKS * CHUNK
SOFTMAX_KEYS = (LEFT_CHUNKS + 1) * CHUNK + (LANES - CHUNK)
BIAS_ROW_LEN = Q_BLOCK + K_BLOCK
CONV_HALO = 32
CONV_ROWS = 64
CONV_TILE_LAG = 3

BF16 = jnp.bfloat16
F32 = jnp.float32


def _rms(x, g):
    ms = jnp.mean(x * x, axis=-1, keepdims=True)
    return x * lax.rsqrt(ms + EPS) * g


def _mm(a, w):
    return lax.dot_general(a, w, (((1,), (0,)), ((), ())), preferred_element_type=F32)


def _params():
    return pltpu.CompilerParams(
        dimension_semantics=("arbitrary",), vmem_limit_bytes=VMEM_LIMIT_BYTES)


def _const_spec(shape):
    nd = len(shape)
    return pl.BlockSpec(shape, lambda i: (0,) * nd, pipeline_mode=pl.Buffered(1))


def _row_spec(rows, cols):
    return pl.BlockSpec((rows, cols), lambda i: (i, 0))


def _stage_weight(src_hbm, dst_ref, stage_ref, sem):
    rows = stage_ref.shape[1]
    n = src_hbm.shape[0] // rows

    def copy(i):
        return pltpu.make_async_copy(
            src_hbm.at[pl.ds(i * rows, rows)], stage_ref.at[i % 2], sem.at[i % 2])

    copy(0).start()
    for i in range(n):
        if i + 1 < n:
            copy(i + 1).start()
        copy(i).wait()
        dst_ref[pl.ds(i * rows, rows), :] = stage_ref[i % 2].astype(BF16)


def _ffn_kernel(x_ref, gpre_ref, wg_hbm, wu_hbm, wd_hbm, gpost_ref, o_ref,
                wg_ref, wu_ref, wd_ref, stage_in_ref, stage_out_ref, sem):
    @pl.when(pl.program_id(0) == 0)
    def _():
        _stage_weight(wg_hbm, wg_ref, stage_in_ref, sem)
        _stage_weight(wu_hbm, wu_ref, stage_in_ref, sem)
        _stage_weight(wd_hbm, wd_ref, stage_out_ref, sem)

    halves = [slice(r0, r0 + FFN_PART) for r0 in range(0, FFN_ROWS, FFN_PART)]
    xn = [_rms(x_ref[rows, :], gpre_ref[...]).astype(BF16) for rows in halves]
    act = []
    for xh in xn:
        g = _mm(xh, wg_ref[...])
        u = _mm(xh, wu_ref[...])
        act.append((g * jax.nn.sigmoid(g) * u).astype(BF16))
    f = [_mm(a, wd_ref[...]) for a in act]
    for rows, fh in zip(halves, f):
        o_ref[rows, :] = x_ref[rows, :] + 0.5 * _rms(fh, gpost_ref[...])


def _ffn(x, gpre, wg, wu, wd, gpost):
    t, d = x.shape
    dff = wg.shape[1]
    return pl.pallas_call(
        _ffn_kernel,
        grid=(t // FFN_ROWS,),
        in_specs=[
            _row_spec(FFN_ROWS, d),
            _const_spec((1, d)),
            pl.BlockSpec(memory_space=pl.ANY),
            pl.BlockSpec(memory_space=pl.ANY),
            pl.BlockSpec(memory_space=pl.ANY),
            _const_spec((1, d)),
        ],
        out_specs=_row_spec(FFN_ROWS, d),
        out_shape=jax.ShapeDtypeStruct((t, d), F32),
        scratch_shapes=[
            pltpu.VMEM((d, dff), BF16),
            pltpu.VMEM((d, dff), BF16),
            pltpu.VMEM((dff, d), BF16),
            pltpu.VMEM((2, d // FFN_STAGE_CHUNKS, dff), F32),
            pltpu.VMEM((2, dff // FFN_STAGE_CHUNKS, d), F32),
            pltpu.SemaphoreType.DMA((2,)),
        ],
        compiler_params=_params(),
        name="ffn",
    )(x, gpre, wg, wu, wd, gpost)


def _proj_kernel(tiles_per_seq,
                 h_ref, g_ref, win_ref, glub_ref, gateb_ref, dww_ref, dwb_ref, lng_ref, lnb_ref,
                 kvq_ref, ag_ref, w_ref, cbuf_ref, shift_ref, conv_ref):
    tm = h_ref.shape[0]
    k_ref = kvq_ref.at[:, 0:D_ATTN]
    v_ref = kvq_ref.at[:, D_ATTN:2 * D_ATTN]
    q_ref = kvq_ref.at[:, 2 * D_ATTN:3 * D_ATTN]
    act_ref = ag_ref.at[:, 0:D_CONV]
    gate_ref = ag_ref.at[:, D_CONV:]

    @pl.when(pl.program_id(0) == 0)
    def _():
        w_ref[...] = win_ref[...].astype(BF16)

    u = _rms(h_ref[...], g_ref[...]).astype(BF16)

    def seg(lo, hi):
        return _mm(u, w_ref[:, lo:hi])

    @pl.when((pl.program_id(0) % tiles_per_seq) == 0)
    def _():
        cbuf_ref[0:CONV_HALO, :] = jnp.zeros((CONV_HALO, D_CONV), F32)

    a = D_ATTN
    c0 = 3 * a
    ca = seg(c0, c0 + D_CONV) + glub_ref[:, :D_CONV]
    cb = seg(c0 + D_CONV, c0 + 2 * D_CONV) + glub_ref[:, D_CONV:]
    cbuf_ref[CONV_HALO:CONV_HALO + tm, :] = ca * jax.nn.sigmoid(cb)
    g0 = c0 + 2 * D_CONV
    ngate = gate_ref.shape[1]

    def after(x):
        x = x[0:1, 0:LANES]
        return jnp.where(x != x, x, 0.0)

    def q_tile(lo):
        r = seg(lo, lo + MXU_COLS)
        q_ref[:, lo:lo + MXU_COLS] = (r * (HEAD_DIM ** -0.5 * LOG2E)).astype(BF16)
        return r

    def kv_tile(ref, base, lo):
        r = seg(base + lo, base + lo + MXU_COLS)
        ref[:, lo:lo + MXU_COLS] = r.astype(BF16)
        return r

    def gate_tile(lo):
        r = seg(g0 + lo, g0 + lo + MXU_COLS)
        gate_ref[:, lo:lo + MXU_COLS] = (r + gateb_ref[:, lo:lo + MXU_COLS]).astype(BF16)
        return r

    tiles = ([functools.partial(q_tile, lo) for lo in range(0, a, MXU_COLS)]
             + [functools.partial(kv_tile, k_ref, a, lo) for lo in range(0, a, MXU_COLS)]
             + [functools.partial(kv_tile, v_ref, 2 * a, lo) for lo in range(0, a, MXU_COLS)]
             + [functools.partial(gate_tile, lo) for lo in range(0, ngate, MXU_COLS)])

    base = CONV_HALO - (CONV_WIDTH - 1)
    span = CONV_HALO + tm - SUBLANES
    n_blocks = (D_CONV // LANES) * (tm // CONV_ROWS)
    anchor = 0.0
    done = []
    anchored = 0
    blk = 0
    for l0 in range(0, D_CONV, LANES):
        cs = slice(l0, l0 + LANES)
        for sh in range(1, SUBLANES):
            shift_ref[sh, 0:span, :] = cbuf_ref[pl.ds(sh, span), cs]
        for r0 in range(0, tm, CONV_ROWS):
            while len(done) * n_blocks < (blk + 1) * len(tiles):
                done.append(tiles[len(done)]())
            while anchored < len(done) - CONV_TILE_LAG:
                anchor = anchor + after(done[anchored])
                anchored += 1
            acc = jnp.zeros((CONV_ROWS, LANES), F32) + dwb_ref[:, cs]
            for j in range(CONV_WIDTH):
                al, sh = divmod(base + j, SUBLANES)
                if sh == 0:
                    tap = cbuf_ref[pl.ds(r0 + al * SUBLANES, CONV_ROWS), cs]
                else:
                    tap = shift_ref[sh, pl.ds(r0 + al * SUBLANES, CONV_ROWS), :]
                acc = acc + tap * (dww_ref[j:j + 1, cs] + anchor)
            conv_ref[r0:r0 + CONV_ROWS, cs] = acc
            anchor = after(acc)
            blk += 1
    for tile in tiles[len(done):]:
        tile()
    cbuf_ref[0:CONV_HALO, :] = cbuf_ref[tm:tm + CONV_HALO, :]
    cv = conv_ref[...]
    mu = jnp.mean(cv, axis=-1, keepdims=True)
    xc = cv - mu
    var = jnp.mean(xc * xc, axis=-1, keepdims=True)
    ln = xc * lax.rsqrt(var + EPS) * lng_ref[...] + lnb_ref[...]
    act_ref[...] = (ln * jax.nn.sigmoid(ln)).astype(BF16)


def _proj(h, g, win, glub, gateb, dww, dwb, lng, lnb, seq_len):
    t, d = h.shape
    tm = ROW_TILE
    n = t // tm
    din = win.shape[1]
    ngate = gateb.shape[1]
    outs = [jax.ShapeDtypeStruct((t, 3 * D_ATTN), BF16),
            jax.ShapeDtypeStruct((t, D_CONV + ngate), BF16)]
    return pl.pallas_call(
        functools.partial(_proj_kernel, seq_len // tm),
        grid=(n,),
        in_specs=[
            _row_spec(tm, d),
            _const_spec((1, d)),
            _const_spec((d, din)),
            _const_spec((1, 2 * D_CONV)),
            _const_spec((1, ngate)),
            _const_spec(dww.shape),
            _const_spec(dwb.shape),
            _const_spec(lng.shape),
            _const_spec(lnb.shape),
        ],
        out_specs=[_row_spec(tm, 3 * D_ATTN), _row_spec(tm, D_CONV + ngate)],
        out_shape=outs,
        scratch_shapes=[
            pltpu.VMEM((d, din), BF16),
            pltpu.VMEM((CONV_HALO + tm, D_CONV), F32),
            pltpu.VMEM((SUBLANES, CONV_HALO + tm, LANES), F32),
            pltpu.VMEM((tm, D_CONV), F32),
        ],
        compiler_params=_params(),
        name="proj",
    )(h, g, win, glub, gateb, dww, dwb, lng, lnb)


def _mixer_kernel(tiles_per_seq,
                  kvq_ref, kvp_ref, ag_ref, h_ref,
                  brow_ref, wa32_ref, wc32_ref, wo32_ref, gpost_ref,
                  o_ref, bias_ref, s_ref, attn_ref, wa_ref, wc_ref, wo_ref):
    tm, d = h_ref.shape
    kc_ref = kvq_ref.at[:, 0:D_ATTN]
    vc_ref = kvq_ref.at[:, D_ATTN:2 * D_ATTN]
    q_ref = kvq_ref.at[:, 2 * D_ATTN:3 * D_ATTN]
    kp_ref = kvp_ref.at[:, 0:D_ATTN]
    vp_ref = kvp_ref.at[:, D_ATTN:2 * D_ATTN]
    act_ref = ag_ref.at[:, 0:D_CONV]
    gate_ref = ag_ref.at[:, D_CONV:]
    first = (pl.program_id(0) % tiles_per_seq) == 0

    @pl.when(pl.program_id(0) == 0)
    def _():
        wa_ref[...] = wa32_ref[...].astype(BF16)
        wc_ref[...] = wc32_ref[...].astype(BF16)
        wo_ref[...] = wo32_ref[...].astype(BF16)
        r = lax.broadcasted_iota(jnp.int32, (Q_BLOCK, K_BLOCK), 0)
        k = lax.broadcasted_iota(jnp.int32, (Q_BLOCK, K_BLOCK), 1)
        rel = k - r + (r & (CHUNK - 1))
        in_band = jnp.logical_and(rel >= 0, rel < (LEFT_CHUNKS + 1) * CHUNK)
        for hd in range(N_HEADS):
            rows = jnp.broadcast_to(brow_ref[hd:hd + 1, :], (Q_BLOCK, BIAS_ROW_LEN))
            toep = pltpu.roll(rows, 0, 1, stride=1, stride_axis=0)
            bias_ref[hd] = jnp.where(in_band, toep[:, Q_BLOCK:], NEG_INF).T

    lane = lax.broadcasted_iota(jnp.int32, (1, LANES), 1)
    lo_half = lane < HEAD_DIM
    krow = lax.broadcasted_iota(jnp.int32, (2 * tm, LANES), 0)
    klane = lax.broadcasted_iota(jnp.int32, (2 * tm, LANES), 1)
    kneg = jnp.where(jnp.logical_and(first, krow < tm), NEG_INF, 0.0)
    n_pairs = N_HEADS // 2
    per_pair = 2 * (tm // Q_BLOCK)

    def scores(hp):
        sl = slice(LANES * hp, LANES * (hp + 1))
        k2 = jnp.concatenate([kp_ref[:, sl], kc_ref[:, sl]], axis=0)
        q2 = q_ref[:, sl]
        for hh in range(2):
            mine = lo_half if hh == 0 else jnp.logical_not(lo_half)
            spare = HEAD_DIM if hh == 0 else 0
            other = jnp.where(klane == spare, kneg, 0.0).astype(BF16)
            kz = jnp.where(mine, k2, other)
            qz = jnp.where(mine, q2, jnp.ones_like(q2))
            for bi, r0 in enumerate(range(0, tm, Q_BLOCK)):
                s_ref[(hp % 2) * per_pair + hh * (tm // Q_BLOCK) + bi] = lax.dot_general(
                    kz[r0:r0 + K_BLOCK], qz[r0:r0 + Q_BLOCK],
                    (((1,), (1,)), ((), ())), preferred_element_type=F32)

    scores(0)
    for hp in range(n_pairs):
        if hp + 1 < n_pairs:
            scores(hp + 1)
        sl = slice(LANES * hp, LANES * (hp + 1))
        v2 = jnp.concatenate([vp_ref[:, sl], vc_ref[:, sl]], axis=0)
        for hh in range(2):
            hd = 2 * hp + hh
            for bi, r0 in enumerate(range(0, tm, Q_BLOCK)):
                st = s_ref.at[(hp % 2) * per_pair + hh * (tm // Q_BLOCK) + bi]
                cols = []
                sums = []
                for q0 in range(0, Q_BLOCK, LANES):
                    ks = slice(q0, q0 + SOFTMAX_KEYS)
                    qs = slice(q0, q0 + LANES)
                    sc = st[ks, qs] + bias_ref[hd, ks, qs]
                    m = jnp.max(sc, axis=0, keepdims=True)
                    p = jnp.exp2(sc - m)
                    sums.append(jnp.sum(p, axis=0, keepdims=True))
                    pieces = [p.astype(BF16)]
                    if q0 > 0:
                        pieces.insert(0, jnp.zeros((q0, LANES), BF16))
                    if q0 + SOFTMAX_KEYS < K_BLOCK:
                        pieces.append(jnp.zeros((K_BLOCK - q0 - SOFTMAX_KEYS, LANES), BF16))
                    cols.append(jnp.concatenate(pieces, axis=0))
                pt = jnp.concatenate(cols, axis=1)
                ot = lax.dot_general(v2[r0:r0 + K_BLOCK], pt, (((0,), (0,)), ((), ())),
                                     preferred_element_type=F32)
                ot = ot / jnp.concatenate(sums, axis=1)
                attn_ref[hd * HEAD_DIM:(hd + 1) * HEAD_DIM, r0:r0 + Q_BLOCK] = (
                    ot[hh * HEAD_DIM:(hh + 1) * HEAD_DIM].astype(BF16))
    y_a = lax.dot_general(attn_ref[...], wa_ref[...], (((0,), (0,)), ((), ())),
                          preferred_element_type=F32)
    y_b = _mm(act_ref[...], wc_ref[...])

    gate_a = jax.nn.sigmoid(gate_ref[:, :d].astype(F32))
    gate_b = jax.nn.sigmoid(gate_ref[:, d:].astype(F32))
    merged = (gate_a * y_a + gate_b * y_b).astype(BF16)
    out = _mm(merged, wo_ref[...])
    o_ref[...] = h_ref[...] + _rms(out, gpost_ref[...])


def _mixer(kvq, act_gates, h, brow, wa, wc, wo, gpost, seq_len):
    t, d = h.shape
    tm = ROW_TILE
    tiles_per_seq = seq_len // tm
    prev = lambda i: (jnp.maximum(i - 1, 0), 0)
    return pl.pallas_call(
        functools.partial(_mixer_kernel, tiles_per_seq),
        grid=(t // tm,),
        in_specs=[
            _row_spec(tm, 3 * D_ATTN),
            pl.BlockSpec((tm, 2 * D_ATTN), prev),
            _row_spec(tm, act_gates.shape[1]),
            _row_spec(tm, d),
            _const_spec(brow.shape),
            _const_spec(wa.shape),
            _const_spec(wc.shape),
            _const_spec(wo.shape),
            _const_spec(gpost.shape),
        ],
        out_specs=_row_spec(tm, d),
        out_shape=jax.ShapeDtypeStruct((t, d), F32),
        scratch_shapes=[
            pltpu.VMEM((N_HEADS, K_BLOCK, Q_BLOCK), F32),
            pltpu.VMEM((4 * tm // Q_BLOCK, K_BLOCK, Q_BLOCK), F32),
            pltpu.VMEM((D_ATTN, tm), BF16),
            pltpu.VMEM(wa.shape, BF16),
            pltpu.VMEM(wc.shape, BF16),
            pltpu.VMEM(wo.shape, BF16),
        ],
        compiler_params=_params(),
        name="mixer",
    )(kvq, kvq, act_gates, h, brow, wa, wc, wo, gpost)


def _rel_bias_rows(rel_table):
    look_back = LEFT_CHUNKS * CHUNK
    n_far = look_back + Q_BLOCK - REL_CLIP
    n_near = BIAS_ROW_LEN - n_far - (2 * REL_CLIP + 1)
    t = rel_table.astype(F32)
    hd = t.shape[0]
    row = jnp.concatenate([jnp.broadcast_to(t[:, -1:], (hd, n_far)), t[:, ::-1],
                           jnp.broadcast_to(t[:, :1], (hd, n_near))], axis=1)
    return row * LOG2E


def kernel(x, ffn1_norm_pre, ffn1_w_gate, ffn1_w_up, ffn1_w_down, ffn1_norm_post, mix_norm_pre, w_in, gate_bias, rel_table, w_attn_out, conv_glu_bias, conv_dw_w, conv_dw_b, conv_ln_g, conv_ln_b, conv_w_out, w_out, mix_norm_post, ffn2_norm_pre, ffn2_w_gate, ffn2_w_up, ffn2_w_down, ffn2_norm_post):
    b, s, d = x.shape
    assert s % ROW_TILE == 0 and d % LANES == 0
    h = x.reshape(b * s, d)
    depth = w_in.shape[0]
    row = lambda p: p.reshape(1, -1)
    for l in range(depth):
        h = _ffn(h, row(ffn1_norm_pre[l]), ffn1_w_gate[l], ffn1_w_up[l], ffn1_w_down[l],
                 row(ffn1_norm_post[l]))
        kvq, act_gates = _proj(h, row(mix_norm_pre[l]), w_in[l],
                               row(conv_glu_bias[l]), row(gate_bias[l]),
                               conv_dw_w[l].reshape(CONV_WIDTH, D_CONV), row(conv_dw_b[l]),
                               row(conv_ln_g[l]), row(conv_ln_b[l]), s)
        h = _mixer(kvq, act_gates, h, _rel_bias_rows(rel_table[l]), w_attn_out[l],
                   conv_w_out[l], w_out[l], row(mix_norm_post[l]), s)
        h = _ffn(h, row(ffn2_norm_pre[l]), ffn2_w_gate[l], ffn2_w_up[l], ffn2_w_down[l],
                 row(ffn2_norm_post[l]))
    return h.reshape(b, s, d)
```

```python
import functools

import jax
import jax.numpy as jnp
from jax import lax
from jax.experimental import pallas as pl
from jax.experimental.pallas import tpu as pltpu

CHUNK = 64
LEFT_CHUNKS = 8
N_HEADS = 8
HEAD_DIM = 64
D_ATTN = N_HEADS * HEAD_DIM
REL_CLIP = 128
D_CONV = 512
CONV_WIDTH = 31
EPS = 1e-6
NEG_INF = -1e30
LOG2E = 1.4426950408889634

LANES = 128
SUBLANES = 8
MXU_COLS = 256
VMEM_LIMIT_BYTES = 56 * 1024 * 1024

ROW_TILE = LEFT_CHUNKS * CHUNK
FFN_ROWS = 2 * ROW_TILE
FFN_PART = ROW_TILE // 2
FFN_STAGE_CHUNKS = 8
Q_BLOCK = 4 * CHUNK
K_BLOCK = Q_BLOCK + LEFT_CHUNKS * CHUNK
SOFTMAX_KEYS = (LEFT_CHUNKS + 1) * CHUNK + (LANES - CHUNK)
BIAS_ROW_LEN = Q_BLOCK + K_BLOCK
BAND_EDGE = LANES - CHUNK
FAR_KEYS = LEFT_CHUNKS * CHUNK - REL_CLIP
CONV_HALO = 32
CONV_ROWS = 64
CONV_TILE_LAG = 3

BF16 = jnp.bfloat16
F32 = jnp.float32


def _rms(x, g):
    ms = jnp.mean(x * x, axis=-1, keepdims=True)
    return x * lax.rsqrt(ms + EPS) * g


def _mm(a, w):
    return lax.dot_general(a, w, (((1,), (0,)), ((), ())), preferred_element_type=F32)


def _params():
    return pltpu.CompilerParams(
        dimension_semantics=("arbitrary",), vmem_limit_bytes=VMEM_LIMIT_BYTES)


def _const_spec(shape):
    nd = len(shape)
    return pl.BlockSpec(shape, lambda i: (0,) * nd, pipeline_mode=pl.Buffered(1))


def _row_spec(rows, cols):
    return pl.BlockSpec((rows, cols), lambda i: (i, 0))


def _stage_weight(src_hbm, dst_ref, stage_ref, sem):
    rows = stage_ref.shape[1]
    n = src_hbm.shape[0] // rows

    def copy(i):
        return pltpu.make_async_copy(
            src_hbm.at[pl.ds(i * rows, rows)], stage_ref.at[i % 2], sem.at[i % 2])

    copy(0).start()
    for i in range(n):
        if i + 1 < n:
            copy(i + 1).start()
        copy(i).wait()
        dst_ref[pl.ds(i * rows, rows), :] = stage_ref[i % 2].astype(BF16)


def _ffn_kernel(x_ref, gpre_ref, wg_hbm, wu_hbm, wd_hbm, gpost_ref, o_ref,
                wg_ref, wu_ref, wd_ref, stage_in_ref, stage_out_ref, sem):
    @pl.when(pl.program_id(0) == 0)
    def _():
        _stage_weight(wg_hbm, wg_ref, stage_in_ref, sem)
        _stage_weight(wu_hbm, wu_ref, stage_in_ref, sem)
        _stage_weight(wd_hbm, wd_ref, stage_out_ref, sem)

    halves = [slice(r0, r0 + FFN_PART) for r0 in range(0, FFN_ROWS, FFN_PART)]
    xn = [_rms(x_ref[rows, :], gpre_ref[...]).astype(BF16) for rows in halves]
    act = []
    for xh in xn:
        g = _mm(xh, wg_ref[...])
        u = _mm(xh, wu_ref[...])
        act.append((g * jax.nn.sigmoid(g) * u).astype(BF16))
    f = [_mm(a, wd_ref[...]) for a in act]
    for rows, fh in zip(halves, f):
        o_ref[rows, :] = x_ref[rows, :] + 0.5 * _rms(fh, gpost_ref[...])


def _ffn(x, gpre, wg, wu, wd, gpost):
    t, d = x.shape
    dff = wg.shape[1]
    return pl.pallas_call(
        _ffn_kernel,
        grid=(t // FFN_ROWS,),
        in_specs=[
            _row_spec(FFN_ROWS, d),
            _const_spec((1, d)),
            pl.BlockSpec(memory_space=pl.ANY),
            pl.BlockSpec(memory_space=pl.ANY),
            pl.BlockSpec(memory_space=pl.ANY),
            _const_spec((1, d)),
        ],
        out_specs=_row_spec(FFN_ROWS, d),
        out_shape=jax.ShapeDtypeStruct((t, d), F32),
        scratch_shapes=[
            pltpu.VMEM((d, dff), BF16),
            pltpu.VMEM((d, dff), BF16),
            pltpu.VMEM((dff, d), BF16),
            pltpu.VMEM((2, d // FFN_STAGE_CHUNKS, dff), F32),
            pltpu.VMEM((2, dff // FFN_STAGE_CHUNKS, d), F32),
            pltpu.SemaphoreType.DMA((2,)),
        ],
        compiler_params=_params(),
        name="ffn",
    )(x, gpre, wg, wu, wd, gpost)


def _proj_kernel(tiles_per_seq,
                 h_ref, g_ref, win_ref, glub_ref, gateb_ref, dww_ref, dwb_ref, lng_ref, lnb_ref,
                 kvq_ref, ag_ref, w_ref, cbuf_ref, shift_ref, conv_ref):
    tm = h_ref.shape[0]
    k_ref = kvq_ref.at[:, 0:D_ATTN]
    v_ref = kvq_ref.at[:, D_ATTN:2 * D_ATTN]
    q_ref = kvq_ref.at[:, 2 * D_ATTN:3 * D_ATTN]
    act_ref = ag_ref.at[:, 0:D_CONV]
    gate_ref = ag_ref.at[:, D_CONV:]

    @pl.when(pl.program_id(0) == 0)
    def _():
        w_ref[...] = win_ref[...].astype(BF16)

    u = _rms(h_ref[...], g_ref[...]).astype(BF16)

    def seg(lo, hi):
        return _mm(u, w_ref[:, lo:hi])

    @pl.when((pl.program_id(0) % tiles_per_seq) == 0)
    def _():
        cbuf_ref[0:CONV_HALO, :] = jnp.zeros((CONV_HALO, D_CONV), F32)

    a = D_ATTN
    c0 = 3 * a
    ca = seg(c0, c0 + D_CONV) + glub_ref[:, :D_CONV]
    cb = seg(c0 + D_CONV, c0 + 2 * D_CONV) + glub_ref[:, D_CONV:]
    cbuf_ref[CONV_HALO:CONV_HALO + tm, :] = ca * jax.nn.sigmoid(cb)
    g0 = c0 + 2 * D_CONV
    ngate = gate_ref.shape[1]

    def after(x):
        x = x[0:1, 0:LANES]
        return jnp.where(x != x, x, 0.0)

    def q_tile(lo):
        r = seg(lo, lo + MXU_COLS)
        q_ref[:, lo:lo + MXU_COLS] = (r * (HEAD_DIM ** -0.5 * LOG2E)).astype(BF16)
        return r

    def kv_tile(ref, base, lo):
        r = seg(base + lo, base + lo + MXU_COLS)
        ref[:, lo:lo + MXU_COLS] = r.astype(BF16)
        return r

    def gate_tile(lo):
        r = seg(g0 + lo, g0 + lo + MXU_COLS)
        gate_ref[:, lo:lo + MXU_COLS] = (r + gateb_ref[:, lo:lo + MXU_COLS]).astype(BF16)
        return r

    tiles = ([functools.partial(q_tile, lo) for lo in range(0, a, MXU_COLS)]
             + [functools.partial(kv_tile, k_ref, a, lo) for lo in range(0, a, MXU_COLS)]
             + [functools.partial(kv_tile, v_ref, 2 * a, lo) for lo in range(0, a, MXU_COLS)]
             + [functools.partial(gate_tile, lo) for lo in range(0, ngate, MXU_COLS)])

    base = CONV_HALO - (CONV_WIDTH - 1)
    span = CONV_HALO + tm - SUBLANES
    n_blocks = (D_CONV // LANES) * (tm // CONV_ROWS)
    anchor = 0.0
    done = []
    anchored = 0
    blk = 0
    for l0 in range(0, D_CONV, LANES):
        cs = slice(l0, l0 + LANES)
        for sh in range(1, SUBLANES):
            shift_ref[sh, 0:span, :] = cbuf_ref[pl.ds(sh, span), cs]
        for r0 in range(0, tm, CONV_ROWS):
            while len(done) * n_blocks < (blk + 1) * len(tiles):
                done.append(tiles[len(done)]())
            while anchored < len(done) - CONV_TILE_LAG:
                anchor = anchor + after(done[anchored])
                anchored += 1
            acc = jnp.zeros((CONV_ROWS, LANES), F32) + dwb_ref[:, cs]
            for j in range(CONV_WIDTH):
                al, sh = divmod(base + j, SUBLANES)
                if sh == 0:
                    tap = cbuf_ref[pl.ds(r0 + al * SUBLANES, CONV_ROWS), cs]
                else:
                    tap = shift_ref[sh, pl.ds(r0 + al * SUBLANES, CONV_ROWS), :]
                acc = acc + tap * (dww_ref[j:j + 1, cs] + anchor)
            conv_ref[r0:r0 + CONV_ROWS, cs] = acc
            anchor = after(acc)
            blk += 1
    for tile in tiles[len(done):]:
        tile()
    cbuf_ref[0:CONV_HALO, :] = cbuf_ref[tm:tm + CONV_HALO, :]
    cv = conv_ref[...]
    mu = jnp.mean(cv, axis=-1, keepdims=True)
    xc = cv - mu
    var = jnp.mean(xc * xc, axis=-1, keepdims=True)
    ln = xc * lax.rsqrt(var + EPS) * lng_ref[...] + lnb_ref[...]
    act_ref[...] = (ln * jax.nn.sigmoid(ln)).astype(BF16)


def _proj(h, g, win, glub, gateb, dww, dwb, lng, lnb, seq_len):
    t, d = h.shape
    tm = ROW_TILE
    n = t // tm
    din = win.shape[1]
    ngate = gateb.shape[1]
    outs = [jax.ShapeDtypeStruct((t, 3 * D_ATTN), BF16),
            jax.ShapeDtypeStruct((t, D_CONV + ngate), BF16)]
    return pl.pallas_call(
        functools.partial(_proj_kernel, seq_len // tm),
        grid=(n,),
        in_specs=[
            _row_spec(tm, d),
            _const_spec((1, d)),
            _const_spec((d, din)),
            _const_spec((1, 2 * D_CONV)),
            _const_spec((1, ngate)),
            _const_spec(dww.shape),
            _const_spec(dwb.shape),
            _const_spec(lng.shape),
            _const_spec(lnb.shape),
        ],
        out_specs=[_row_spec(tm, 3 * D_ATTN), _row_spec(tm, D_CONV + ngate)],
        out_shape=outs,
        scratch_shapes=[
            pltpu.VMEM((d, din), BF16),
            pltpu.VMEM((CONV_HALO + tm, D_CONV), F32),
            pltpu.VMEM((SUBLANES, CONV_HALO + tm, LANES), F32),
            pltpu.VMEM((tm, D_CONV), F32),
        ],
        compiler_params=_params(),
        name="proj",
    )(h, g, win, glub, gateb, dww, dwb, lng, lnb)


def _mixer_kernel(tiles_per_seq,
                  kvq_ref, kvp_ref, ag_ref, h_ref,
                  brow_ref, wa32_ref, wc32_ref, wo32_ref, gpost_ref,
                  o_ref, bias_ref, s_ref, attn_ref, wa_ref, wc_ref, wo_ref):
    tm, d = h_ref.shape
    kc_ref = kvq_ref.at[:, 0:D_ATTN]
    vc_ref = kvq_ref.at[:, D_ATTN:2 * D_ATTN]
    q_ref = kvq_ref.at[:, 2 * D_ATTN:3 * D_ATTN]
    kp_ref = kvp_ref.at[:, 0:D_ATTN]
    vp_ref = kvp_ref.at[:, D_ATTN:2 * D_ATTN]
    act_ref = ag_ref.at[:, 0:D_CONV]
    gate_ref = ag_ref.at[:, D_CONV:]
    first = (pl.program_id(0) % tiles_per_seq) == 0

    @pl.when(pl.program_id(0) == 0)
    def _():
        wa_ref[...] = wa32_ref[...].astype(BF16)
        wc_ref[...] = wc32_ref[...].astype(BF16)
        wo_ref[...] = wo32_ref[...].astype(BF16)
        r = lax.broadcasted_iota(jnp.int32, (Q_BLOCK, K_BLOCK), 0)
        k = lax.broadcasted_iota(jnp.int32, (Q_BLOCK, K_BLOCK), 1)
        rel = k - r + (r & (CHUNK - 1))
        in_band = jnp.logical_and(rel >= 0, rel < (LEFT_CHUNKS + 1) * CHUNK)
        for hd in range(N_HEADS):
            rows = jnp.broadcast_to(brow_ref[hd:hd + 1, :], (Q_BLOCK, BIAS_ROW_LEN))
            toep = pltpu.roll(rows, 0, 1, stride=1, stride_axis=0)
            far = brow_ref[hd:hd + 1, 0:1]
            bias_ref[hd] = jnp.where(in_band, toep[:, Q_BLOCK:] - far, NEG_INF).T

    lane = lax.broadcasted_iota(jnp.int32, (1, LANES), 1)
    lo_half = lane < HEAD_DIM
    krow = lax.broadcasted_iota(jnp.int32, (2 * tm, LANES), 0)
    klane = lax.broadcasted_iota(jnp.int32, (2 * tm, LANES), 1)
    kneg = jnp.where(jnp.logical_and(first, krow < tm), NEG_INF, 0.0)
    n_pairs = N_HEADS // 2
    per_pair = 2 * (tm // Q_BLOCK)

    def scores(hp):
        sl = slice(LANES * hp, LANES * (hp + 1))
        k2 = jnp.concatenate([kp_ref[:, sl], kc_ref[:, sl]], axis=0)
        q2 = q_ref[:, sl]
        for hh in range(2):
            mine = lo_half if hh == 0 else jnp.logical_not(lo_half)
            spare = HEAD_DIM if hh == 0 else 0
            other = jnp.where(klane == spare, kneg, 0.0).astype(BF16)
            kz = jnp.where(mine, k2, other)
            qz = jnp.where(mine, q2, jnp.ones_like(q2))
            for bi, r0 in enumerate(range(0, tm, Q_BLOCK)):
                s_ref[(hp % 2) * per_pair + hh * (tm // Q_BLOCK) + bi] = lax.dot_general(
                    kz[r0:r0 + K_BLOCK], qz[r0:r0 + Q_BLOCK],
                    (((1,), (1,)), ((), ())), preferred_element_type=F32)

    scores(0)
    for hp in range(n_pairs):
        if hp + 1 < n_pairs:
            scores(hp + 1)
        sl = slice(LANES * hp, LANES * (hp + 1))
        v2 = jnp.concatenate([vp_ref[:, sl], vc_ref[:, sl]], axis=0)
        for hh in range(2):
            hd = 2 * hp + hh
            for bi, r0 in enumerate(range(0, tm, Q_BLOCK)):
                st = s_ref.at[(hp % 2) * per_pair + hh * (tm // Q_BLOCK) + bi]
                cols = []
                sums = []
                for q0 in range(0, Q_BLOCK, LANES):
                    ks = slice(q0, q0 + SOFTMAX_KEYS)
                    qs = slice(q0, q0 + LANES)
                    lo_rows = slice(q0, q0 + BAND_EDGE)
                    mid_rows = slice(q0 + BAND_EDGE, q0 + FAR_KEYS)
                    hi_rows = slice(q0 + FAR_KEYS, q0 + SOFTMAX_KEYS)
                    sc = jnp.concatenate([st[lo_rows, qs] + bias_ref[hd, lo_rows, qs],
                                          st[mid_rows, qs],
                                          st[hi_rows, qs] + bias_ref[hd, hi_rows, qs]], axis=0)
                    m = jnp.max(sc, axis=0, keepdims=True)
                    p = jnp.exp2(sc - m)
                    sums.append(jnp.sum(p, axis=0, keepdims=True))
                    pieces = [p.astype(BF16)]
                    if q0 > 0:
                        pieces.insert(0, jnp.zeros((q0, LANES), BF16))
                    if q0 + SOFTMAX_KEYS < K_BLOCK:
                        pieces.append(jnp.zeros((K_BLOCK - q0 - SOFTMAX_KEYS, LANES), BF16))
                    cols.append(jnp.concatenate(pieces, axis=0))
                pt = jnp.concatenate(cols, axis=1)
                ot = lax.dot_general(v2[r0:r0 + K_BLOCK], pt, (((0,), (0,)), ((), ())),
                                     preferred_element_type=F32)
                ot = ot / jnp.concatenate(sums, axis=1)
                attn_ref[hd * HEAD_DIM:(hd + 1) * HEAD_DIM, r0:r0 + Q_BLOCK] = (
                    ot[hh * HEAD_DIM:(hh + 1) * HEAD_DIM].astype(BF16))
    y_a = lax.dot_general(attn_ref[...], wa_ref[...], (((0,), (0,)), ((), ())),
                          preferred_element_type=F32)
    y_b = _mm(act_ref[...], wc_ref[...])

    gate_a = jax.nn.sigmoid(gate_ref[:, :d].astype(F32))
    gate_b = jax.nn.sigmoid(gate_ref[:, d:].astype(F32))
    merged = (gate_a * y_a + gate_b * y_b).astype(BF16)
    out = _mm(merged, wo_ref[...])
    o_ref[...] = h_ref[...] + _rms(out, gpost_ref[...])


def _mixer(kvq, act_gates, h, brow, wa, wc, wo, gpost, seq_len):
    t, d = h.shape
    tm = ROW_TILE
    tiles_per_seq = seq_len // tm
    prev = lambda i: (jnp.maximum(i - 1, 0), 0)
    return pl.pallas_call(
        functools.partial(_mixer_kernel, tiles_per_seq),
        grid=(t // tm,),
        in_specs=[
            _row_spec(tm, 3 * D_ATTN),
            pl.BlockSpec((tm, 2 * D_ATTN), prev),
            _row_spec(tm, act_gates.shape[1]),
            _row_spec(tm, d),
            _const_spec(brow.shape),
            _const_spec(wa.shape),
            _const_spec(wc.shape),
            _const_spec(wo.shape),
            _const_spec(gpost.shape),
        ],
        out_specs=_row_spec(tm, d),
        out_shape=jax.ShapeDtypeStruct((t, d), F32),
        scratch_shapes=[
            pltpu.VMEM((N_HEADS, K_BLOCK, Q_BLOCK), F32),
            pltpu.VMEM((4 * tm // Q_BLOCK, K_BLOCK, Q_BLOCK), F32),
            pltpu.VMEM((D_ATTN, tm), BF16),
            pltpu.VMEM(wa.shape, BF16),
            pltpu.VMEM(wc.shape, BF16),
            pltpu.VMEM(wo.shape, BF16),
        ],
        compiler_params=_params(),
        name="mixer",
    )(kvq, kvq, act_gates, h, brow, wa, wc, wo, gpost)


def _rel_bias_rows(rel_table):
    look_back = LEFT_CHUNKS * CHUNK
    n_far = look_back + Q_BLOCK - REL_CLIP
    n_near = BIAS_ROW_LEN - n_far - (2 * REL_CLIP + 1)
    t = rel_table.astype(F32)
    hd = t.shape[0]
    row = jnp.concatenate([jnp.broadcast_to(t[:, -1:], (hd, n_far)), t[:, ::-1],
                           jnp.broadcast_to(t[:, :1], (hd, n_near))], axis=1)
    return row * LOG2E


def kernel(x, ffn1_norm_pre, ffn1_w_gate, ffn1_w_up, ffn1_w_down, ffn1_norm_post, mix_norm_pre, w_in, gate_bias, rel_table, w_attn_out, conv_glu_bias, conv_dw_w, conv_dw_b, conv_ln_g, conv_ln_b, conv_w_out, w_out, mix_norm_post, ffn2_norm_pre, ffn2_w_gate, ffn2_w_up, ffn2_w_down, ffn2_norm_post):
    b, s, d = x.shape
    assert s % ROW_TILE == 0 and d % LANES == 0
    h = x.reshape(b * s, d)
    depth = w_in.shape[0]
    row = lambda p: p.reshape(1, -1)
    for l in range(depth):
        h = _ffn(h, row(ffn1_norm_pre[l]), ffn1_w_gate[l], ffn1_w_up[l], ffn1_w_down[l],
                 row(ffn1_norm_post[l]))
        kvq, act_gates = _proj(h, row(mix_norm_pre[l]), w_in[l],
                               row(conv_glu_bias[l]), row(gate_bias[l]),
                               conv_dw_w[l].reshape(CONV_WIDTH, D_CONV), row(conv_dw_b[l]),
                               row(conv_ln_g[l]), row(conv_ln_b[l]), s)
        h = _mixer(kvq, act_gates, h, _rel_bias_rows(rel_table[l]), w_attn_out[l],
                   conv_w_out[l], w_out[l], row(mix_norm_post[l]), s)
        h = _ffn(h, row(ffn2_norm_pre[l]), ffn2_w_gate[l], ffn2_w_up[l], ffn2_w_down[l],
                 row(ffn2_norm_post[l]))
    return h.reshape(b, s, d)
```

```python
import functools

import jax
import jax.numpy as jnp
from jax import lax
from jax.experimental import pallas as pl
from jax.experimental.pallas import tpu as pltpu

CHUNK = 64
LEFT_CHUNKS = 8
N_HEADS = 8
HEAD_DIM = 64
D_ATTN = N_HEADS * HEAD_DIM
REL_CLIP = 128
D_CONV = 512
CONV_WIDTH = 31
EPS = 1e-6
NEG_INF = -1e30
LOG2E = 1.4426950408889634

LANES = 128
SUBLANES = 8
MXU_COLS = 256
VMEM_LIMIT_BYTES = 56 * 1024 * 1024

ROW_TILE = LEFT_CHUNKS * CHUNK
FFN_ROWS = 2 * ROW_TILE
FFN_PART = ROW_TILE // 2
FFN_STAGE_CHUNKS = 8
Q_BLOCK = 4 * CHUNK
K_BLOCK = Q_BLOCK + LEFT_CHUNKS * CHUNK
SOFTMAX_KEYS = (LEFT_CHUNKS + 1) * CHUNK + (LANES - CHUNK)
BIAS_ROW_LEN = Q_BLOCK + K_BLOCK
BAND_EDGE = LANES - CHUNK
FAR_KEYS = LEFT_CHUNKS * CHUNK - REL_CLIP
CONV_HALO = 32
CONV_ROWS = 64
CONV_TILE_LAG = 3

BF16 = jnp.bfloat16
F32 = jnp.float32


def _rms(x, g):
    ms = jnp.mean(x * x, axis=-1, keepdims=True)
    return x * lax.rsqrt(ms + EPS) * g


def _mm(a, w):
    return lax.dot_general(a, w, (((1,), (0,)), ((), ())), preferred_element_type=F32)


def _params():
    return pltpu.CompilerParams(
        dimension_semantics=("arbitrary",), vmem_limit_bytes=VMEM_LIMIT_BYTES)


def _const_spec(shape):
    nd = len(shape)
    return pl.BlockSpec(shape, lambda i: (0,) * nd, pipeline_mode=pl.Buffered(1))


def _row_spec(rows, cols):
    return pl.BlockSpec((rows, cols), lambda i: (i, 0))


def _stage_weight(src_hbm, dst_ref, stage_ref, sem):
    rows = stage_ref.shape[1]
    n = src_hbm.shape[0] // rows

    def copy(i):
        return pltpu.make_async_copy(
            src_hbm.at[pl.ds(i * rows, rows)], stage_ref.at[i % 2], sem.at[i % 2])

    copy(0).start()
    for i in range(n):
        if i + 1 < n:
            copy(i + 1).start()
        copy(i).wait()
        dst_ref[pl.ds(i * rows, rows), :] = stage_ref[i % 2].astype(BF16)


def _ffn_kernel(x_ref, gpre_ref, wg_hbm, wu_hbm, wd_hbm, gpost_ref, o_ref,
                wg_ref, wu_ref, wd_ref, stage_in_ref, stage_out_ref, sem):
    @pl.when(pl.program_id(0) == 0)
    def _():
        _stage_weight(wg_hbm, wg_ref, stage_in_ref, sem)
        _stage_weight(wu_hbm, wu_ref, stage_in_ref, sem)
        _stage_weight(wd_hbm, wd_ref, stage_out_ref, sem)

    halves = [slice(r0, r0 + FFN_PART) for r0 in range(0, FFN_ROWS, FFN_PART)]
    xn = [_rms(x_ref[rows, :], gpre_ref[...]).astype(BF16) for rows in halves]
    act = []
    for xh in xn:
        g = _mm(xh, wg_ref[...])
        u = _mm(xh, wu_ref[...])
        act.append((g * jax.nn.sigmoid(g) * u).astype(BF16))
    f = [_mm(a, wd_ref[...]) for a in act]
    for rows, fh in zip(halves, f):
        o_ref[rows, :] = x_ref[rows, :] + 0.5 * _rms(fh, gpost_ref[...])


def _ffn(x, gpre, wg, wu, wd, gpost):
    t, d = x.shape
    dff = wg.shape[1]
    return pl.pallas_call(
        _ffn_kernel,
        grid=(t // FFN_ROWS,),
        in_specs=[
            _row_spec(FFN_ROWS, d),
            _const_spec((1, d)),
            pl.BlockSpec(memory_space=pl.ANY),
            pl.BlockSpec(memory_space=pl.ANY),
            pl.BlockSpec(memory_space=pl.ANY),
            _const_spec((1, d)),
        ],
        out_specs=_row_spec(FFN_ROWS, d),
        out_shape=jax.ShapeDtypeStruct((t, d), F32),
        scratch_shapes=[
            pltpu.VMEM((d, dff), BF16),
            pltpu.VMEM((d, dff), BF16),
            pltpu.VMEM((dff, d), BF16),
            pltpu.VMEM((2, d // FFN_STAGE_CHUNKS, dff), F32),
            pltpu.VMEM((2, dff // FFN_STAGE_CHUNKS, d), F32),
            pltpu.SemaphoreType.DMA((2,)),
        ],
        compiler_params=_params(),
        name="ffn",
    )(x, gpre, wg, wu, wd, gpost)


def _proj_kernel(tiles_per_seq,
                 h_ref, g_ref, win_ref, glub_ref, gateb_ref, dww_ref, dwb_ref, lng_ref, lnb_ref,
                 kvq_ref, ag_ref, w_ref, cbuf_ref, shift_ref, conv_ref):
    tm = h_ref.shape[0]
    k_ref = kvq_ref.at[:, 0:D_ATTN]
    v_ref = kvq_ref.at[:, D_ATTN:2 * D_ATTN]
    q_ref = kvq_ref.at[:, 2 * D_ATTN:3 * D_ATTN]
    act_ref = ag_ref.at[:, 0:D_CONV]
    gate_ref = ag_ref.at[:, D_CONV:]

    @pl.when(pl.program_id(0) == 0)
    def _():
        w_ref[...] = win_ref[...].astype(BF16)

    u = _rms(h_ref[...], g_ref[...]).astype(BF16)

    def seg(lo, hi):
        return _mm(u, w_ref[:, lo:hi])

    @pl.when((pl.program_id(0) % tiles_per_seq) == 0)
    def _():
        cbuf_ref[0:CONV_HALO, :] = jnp.zeros((CONV_HALO, D_CONV), F32)

    a = D_ATTN
    c0 = 3 * a
    ca = seg(c0, c0 + D_CONV) + glub_ref[:, :D_CONV]
    cb = seg(c0 + D_CONV, c0 + 2 * D_CONV) + glub_ref[:, D_CONV:]
    cbuf_ref[CONV_HALO:CONV_HALO + tm, :] = ca * jax.nn.sigmoid(cb)
    g0 = c0 + 2 * D_CONV
    ngate = gate_ref.shape[1]

    def after(x):
        x = x[0:1, 0:LANES]
        return jnp.where(x != x, x, 0.0)

    def q_tile(lo):
        r = seg(lo, lo + MXU_COLS)
        q_ref[:, lo:lo + MXU_COLS] = (r * (HEAD_DIM ** -0.5 * LOG2E)).astype(BF16)
        return r

    def kv_tile(ref, base, lo):
        r = seg(base + lo, base + lo + MXU_COLS)
        ref[:, lo:lo + MXU_COLS] = r.astype(BF16)
        return r

    def gate_tile(lo):
        r = seg(g0 + lo, g0 + lo + MXU_COLS)
        gate_ref[:, lo:lo + MXU_COLS] = (r + gateb_ref[:, lo:lo + MXU_COLS]).astype(BF16)
        return r

    tiles = ([functools.partial(q_tile, lo) for lo in range(0, a, MXU_COLS)]
             + [functools.partial(kv_tile, k_ref, a, lo) for lo in range(0, a, MXU_COLS)]
             + [functools.partial(kv_tile, v_ref, 2 * a, lo) for lo in range(0, a, MXU_COLS)]
             + [functools.partial(gate_tile, lo) for lo in range(0, ngate, MXU_COLS)])

    base = CONV_HALO - (CONV_WIDTH - 1)
    span = CONV_HALO + tm - SUBLANES
    n_blocks = (D_CONV // LANES) * (tm // CONV_ROWS)
    anchor = 0.0
    done = []
    anchored = 0
    blk = 0
    for l0 in range(0, D_CONV, LANES):
        cs = slice(l0, l0 + LANES)
        for sh in range(1, SUBLANES):
            shift_ref[sh, 0:span, :] = cbuf_ref[pl.ds(sh, span), cs]
        for r0 in range(0, tm, CONV_ROWS):
            while len(done) * n_blocks < (blk + 1) * len(tiles):
                done.append(tiles[len(done)]())
            while anchored < len(done) - CONV_TILE_LAG:
                anchor = anchor + after(done[anchored])
                anchored += 1
            acc = jnp.zeros((CONV_ROWS, LANES), F32) + dwb_ref[:, cs]
            for j in range(CONV_WIDTH):
                al, sh = divmod(base + j, SUBLANES)
                if sh == 0:
                    tap = cbuf_ref[pl.ds(r0 + al * SUBLANES, CONV_ROWS), cs]
                else:
                    tap = shift_ref[sh, pl.ds(r0 + al * SUBLANES, CONV_ROWS), :]
                acc = acc + tap * (dww_ref[j:j + 1, cs] + anchor)
            conv_ref[r0:r0 + CONV_ROWS, cs] = acc
            anchor = after(acc)
            blk += 1
    for tile in tiles[len(done):]:
        tile()
    cbuf_ref[0:CONV_HALO, :] = cbuf_ref[tm:tm + CONV_HALO, :]
    cv = conv_ref[...]
    mu = jnp.mean(cv, axis=-1, keepdims=True)
    xc = cv - mu
    var = jnp.mean(xc * xc, axis=-1, keepdims=True)
    ln = xc * lax.rsqrt(var + EPS) * lng_ref[...] + lnb_ref[...]
    act_ref[...] = (ln * jax.nn.sigmoid(ln)).astype(BF16)


def _proj(h, g, win, glub, gateb, dww, dwb, lng, lnb, seq_len):
    t, d = h.shape
    tm = ROW_TILE
    n = t // tm
    din = win.shape[1]
    ngate = gateb.shape[1]
    outs = [jax.ShapeDtypeStruct((t, 3 * D_ATTN), BF16),
            jax.ShapeDtypeStruct((t, D_CONV + ngate), BF16)]
    return pl.pallas_call(
        functools.partial(_proj_kernel, seq_len // tm),
        grid=(n,),
        in_specs=[
            _row_spec(tm, d),
            _const_spec((1, d)),
            _const_spec((d, din)),
            _const_spec((1, 2 * D_CONV)),
            _const_spec((1, ngate)),
            _const_spec(dww.shape),
            _const_spec(dwb.shape),
            _const_spec(lng.shape),
            _const_spec(lnb.shape),
        ],
        out_specs=[_row_spec(tm, 3 * D_ATTN), _row_spec(tm, D_CONV + ngate)],
        out_shape=outs,
        scratch_shapes=[
            pltpu.VMEM((d, din), BF16),
            pltpu.VMEM((CONV_HALO + tm, D_CONV), F32),
            pltpu.VMEM((SUBLANES, CONV_HALO + tm, LANES), F32),
            pltpu.VMEM((tm, D_CONV), F32),
        ],
        compiler_params=_params(),
        name="proj",
    )(h, g, win, glub, gateb, dww, dwb, lng, lnb)


def _mixer_kernel(tiles_per_seq,
                  kvq_ref, kvp_ref, ag_ref, h_ref,
                  brow_ref, wa32_ref, wc32_ref, wo32_ref, gpost_ref,
                  o_ref, bias_ref, s_ref, attn_ref, wa_ref, wc_ref, wo_ref):
    tm, d = h_ref.shape
    kc_ref = kvq_ref.at[:, 0:D_ATTN]
    vc_ref = kvq_ref.at[:, D_ATTN:2 * D_ATTN]
    q_ref = kvq_ref.at[:, 2 * D_ATTN:3 * D_ATTN]
    kp_ref = kvp_ref.at[:, 0:D_ATTN]
    vp_ref = kvp_ref.at[:, D_ATTN:2 * D_ATTN]
    act_ref = ag_ref.at[:, 0:D_CONV]
    gate_ref = ag_ref.at[:, D_CONV:]
    first = (pl.program_id(0) % tiles_per_seq) == 0

    @pl.when(pl.program_id(0) == 0)
    def _():
        wa_ref[...] = wa32_ref[...].astype(BF16)
        wc_ref[...] = wc32_ref[...].astype(BF16)
        wo_ref[...] = wo32_ref[...].astype(BF16)
        r = lax.broadcasted_iota(jnp.int32, (Q_BLOCK, K_BLOCK), 0)
        k = lax.broadcasted_iota(jnp.int32, (Q_BLOCK, K_BLOCK), 1)
        rel = k - r + (r & (CHUNK - 1))
        in_band = jnp.logical_and(rel >= 0, rel < (LEFT_CHUNKS + 1) * CHUNK)
        for hd in range(N_HEADS):
            rows = jnp.broadcast_to(brow_ref[hd:hd + 1, :], (Q_BLOCK, BIAS_ROW_LEN))
            toep = pltpu.roll(rows, 0, 1, stride=1, stride_axis=0)
            far = brow_ref[hd:hd + 1, 0:1]
            bias_ref[hd] = jnp.where(in_band, toep[:, Q_BLOCK:] - far, NEG_INF).T

    lane = lax.broadcasted_iota(jnp.int32, (1, LANES), 1)
    lo_half = lane < HEAD_DIM
    krow = lax.broadcasted_iota(jnp.int32, (2 * tm, LANES), 0)
    klane = lax.broadcasted_iota(jnp.int32, (2 * tm, LANES), 1)
    kneg = jnp.where(jnp.logical_and(first, krow < tm), NEG_INF, 0.0)
    n_pairs = N_HEADS // 2
    per_pair = 2 * (tm // Q_BLOCK)

    def scores(hp):
        sl = slice(LANES * hp, LANES * (hp + 1))
        k2 = jnp.concatenate([kp_ref[:, sl], kc_ref[:, sl]], axis=0)
        q2 = q_ref[:, sl]
        for hh in range(2):
            mine = lo_half if hh == 0 else jnp.logical_not(lo_half)
            spare = HEAD_DIM if hh == 0 else 0
            other = jnp.where(klane == spare, kneg, 0.0).astype(BF16)
            kz = jnp.where(mine, k2, other)
            qz = jnp.where(mine, q2, jnp.ones_like(q2))
            for bi, r0 in enumerate(range(0, tm, Q_BLOCK)):
                s_ref[(hp % 2) * per_pair + hh * (tm // Q_BLOCK) + bi] = lax.dot_general(
                    kz[r0:r0 + K_BLOCK], qz[r0:r0 + Q_BLOCK],
                    (((1,), (1,)), ((), ())), preferred_element_type=F32)

    scores(0)
    for hp in range(n_pairs):
        if hp + 1 < n_pairs:
            scores(hp + 1)
        sl = slice(LANES * hp, LANES * (hp + 1))
        v2 = jnp.concatenate([vp_ref[:, sl], vc_ref[:, sl]], axis=0)
        for hh in range(2):
            hd = 2 * hp + hh
            spare = HEAD_DIM if hh == 0 else 0
            vz = jnp.where(klane == spare, jnp.ones_like(v2), v2)
            for bi, r0 in enumerate(range(0, tm, Q_BLOCK)):
                st = s_ref.at[(hp % 2) * per_pair + hh * (tm // Q_BLOCK) + bi]
                cols = []
                for q0 in range(0, Q_BLOCK, LANES):
                    qs = slice(q0, q0 + LANES)
                    lo_rows = slice(q0, q0 + BAND_EDGE)
                    mid_rows = slice(q0 + BAND_EDGE, q0 + FAR_KEYS)
                    hi_rows = slice(q0 + FAR_KEYS, q0 + SOFTMAX_KEYS)
                    sc = jnp.concatenate([st[lo_rows, qs] + bias_ref[hd, lo_rows, qs],
                                          st[mid_rows, qs],
                                          st[hi_rows, qs] + bias_ref[hd, hi_rows, qs]], axis=0)
                    m = jnp.max(sc, axis=0, keepdims=True)
                    p = jnp.exp2(sc - m)
                    pieces = [p.astype(BF16)]
                    if q0 > 0:
                        pieces.insert(0, jnp.zeros((q0, LANES), BF16))
                    if q0 + SOFTMAX_KEYS < K_BLOCK:
                        pieces.append(jnp.zeros((K_BLOCK - q0 - SOFTMAX_KEYS, LANES), BF16))
                    cols.append(jnp.concatenate(pieces, axis=0))
                pt = jnp.concatenate(cols, axis=1)
                ot = lax.dot_general(vz[r0:r0 + K_BLOCK], pt, (((0,), (0,)), ((), ())),
                                     preferred_element_type=F32)
                ot = ot[hh * HEAD_DIM:(hh + 1) * HEAD_DIM] / ot[spare:spare + 1]
                attn_ref[hd * HEAD_DIM:(hd + 1) * HEAD_DIM, r0:r0 + Q_BLOCK] = ot.astype(BF16)
    y_a = lax.dot_general(attn_ref[...], wa_ref[...], (((0,), (0,)), ((), ())),
                          preferred_element_type=F32)
    y_b = _mm(act_ref[...], wc_ref[...])

    gate_a = jax.nn.sigmoid(gate_ref[:, :d].astype(F32))
    gate_b = jax.nn.sigmoid(gate_ref[:, d:].astype(F32))
    merged = (gate_a * y_a + gate_b * y_b).astype(BF16)
    out = _mm(merged, wo_ref[...])
    o_ref[...] = h_ref[...] + _rms(out, gpost_ref[...])


def _mixer(kvq, act_gates, h, brow, wa, wc, wo, gpost, seq_len):
    t, d = h.shape
    tm = ROW_TILE
    tiles_per_seq = seq_len // tm
    prev = lambda i: (jnp.maximum(i - 1, 0), 0)
    return pl.pallas_call(
        functools.partial(_mixer_kernel, tiles_per_seq),
        grid=(t // tm,),
        in_specs=[
            _row_spec(tm, 3 * D_ATTN),
            pl.BlockSpec((tm, 2 * D_ATTN), prev),
            _row_spec(tm, act_gates.shape[1]),
            _row_spec(tm, d),
            _const_spec(brow.shape),
            _const_spec(wa.shape),
            _const_spec(wc.shape),
            _const_spec(wo.shape),
            _const_spec(gpost.shape),
        ],
        out_specs=_row_spec(tm, d),
        out_shape=jax.ShapeDtypeStruct((t, d), F32),
        scratch_shapes=[
            pltpu.VMEM((N_HEADS, K_BLOCK, Q_BLOCK), F32),
            pltpu.VMEM((4 * tm // Q_BLOCK, K_BLOCK, Q_BLOCK), F32),
            pltpu.VMEM((D_ATTN, tm), BF16),
            pltpu.VMEM(wa.shape, BF16),
            pltpu.VMEM(wc.shape, BF16),
            pltpu.VMEM(wo.shape, BF16),
        ],
        compiler_params=_params(),
        name="mixer",
    )(kvq, kvq, act_gates, h, brow, wa, wc, wo, gpost)


def _rel_bias_rows(rel_table):
    look_back = LEFT_CHUNKS * CHUNK
    n_far = look_back + Q_BLOCK - REL_CLIP
    n_near = BIAS_ROW_LEN - n_far - (2 * REL_CLIP + 1)
    t = rel_table.astype(F32)
    hd = t.shape[0]
    row = jnp.concatenate([jnp.broadcast_to(t[:, -1:], (hd, n_far)), t[:, ::-1],
                           jnp.broadcast_to(t[:, :1], (hd, n_near))], axis=1)
    return row * LOG2E


def kernel(x, ffn1_norm_pre, ffn1_w_gate, ffn1_w_up, ffn1_w_down, ffn1_norm_post, mix_norm_pre, w_in, gate_bias, rel_table, w_attn_out, conv_glu_bias, conv_dw_w, conv_dw_b, conv_ln_g, conv_ln_b, conv_w_out, w_out, mix_norm_post, ffn2_norm_pre, ffn2_w_gate, ffn2_w_up, ffn2_w_down, ffn2_norm_post):
    b, s, d = x.shape
    assert s % ROW_TILE == 0 and d % LANES == 0
    h = x.reshape(b * s, d)
    depth = w_in.shape[0]
    row = lambda p: p.reshape(1, -1)
    for l in range(depth):
        h = _ffn(h, row(ffn1_norm_pre[l]), ffn1_w_gate[l], ffn1_w_up[l], ffn1_w_down[l],
                 row(ffn1_norm_post[l]))
        kvq, act_gates = _proj(h, row(mix_norm_pre[l]), w_in[l],
                               row(conv_glu_bias[l]), row(gate_bias[l]),
                               conv_dw_w[l].reshape(CONV_WIDTH, D_CONV), row(conv_dw_b[l]),
                               row(conv_ln_g[l]), row(conv_ln_b[l]), s)
        h = _mixer(kvq, act_gates, h, _rel_bias_rows(rel_table[l]), w_attn_out[l],
                   conv_w_out[l], w_out[l], row(mix_norm_post[l]), s)
        h = _ffn(h, row(ffn2_norm_pre[l]), ffn2_w_gate[l], ffn2_w_up[l], ffn2_w_down[l],
                 row(ffn2_norm_post[l]))
    return h.reshape(b, s, d)
```

```python
import functools

import jax
import jax.numpy as jnp
from jax import lax
from jax.experimental import pallas as pl
from jax.experimental.pallas import tpu as pltpu

CHUNK = 64
LEFT_CHUNKS = 8
N_HEADS = 8
HEAD_DIM = 64
D_ATTN = N_HEADS * HEAD_DIM
REL_CLIP = 128
D_CONV = 512
CONV_WIDTH = 31
EPS = 1e-6
NEG_INF = -1e30
LOG2E = 1.4426950408889634

LANES = 128
SUBLANES = 8
MXU_COLS = 256
VMEM_LIMIT_BYTES = 56 * 1024 * 1024

ROW_TILE = LEFT_CHUNKS * CHUNK
FFN_ROWS = 2 * ROW_TILE
FFN_PART = ROW_TILE // 2
FFN_STAGE_CHUNKS = 8
Q_BLOCK = 4 * CHUNK
K_BLOCK = Q_BLOCK + LEFT_CHUNKS * CHUNK
SOFTMAX_KEYS = (LEFT_CHUNKS + 1) * CHUNK + (LANES - CHUNK)
BIAS_ROW_LEN = Q_BLOCK + K_BLOCK
BAND_EDGE = LANES - CHUNK
FAR_KEYS = LEFT_CHUNKS * CHUNK - REL_CLIP
SCORE_SLOTS = 4
CONV_HALO = 32
CONV_ROWS = 64
CONV_TILE_LAG = 3

BF16 = jnp.bfloat16
F32 = jnp.float32


def _rms(x, g):
    ms = jnp.mean(x * x, axis=-1, keepdims=True)
    return x * lax.rsqrt(ms + EPS) * g


def _mm(a, w):
    return lax.dot_general(a, w, (((1,), (0,)), ((), ())), preferred_element_type=F32)


def _params():
    return pltpu.CompilerParams(
        dimension_semantics=("arbitrary",), vmem_limit_bytes=VMEM_LIMIT_BYTES)


def _const_spec(shape):
    nd = len(shape)
    return pl.BlockSpec(shape, lambda i: (0,) * nd, pipeline_mode=pl.Buffered(1))


def _row_spec(rows, cols):
    return pl.BlockSpec((rows, cols), lambda i: (i, 0))


def _stage_weight(src_hbm, dst_ref, stage_ref, sem):
    rows = stage_ref.shape[1]
    n = src_hbm.shape[0] // rows

    def copy(i):
        return pltpu.make_async_copy(
            src_hbm.at[pl.ds(i * rows, rows)], stage_ref.at[i % 2], sem.at[i % 2])

    copy(0).start()
    for i in range(n):
        if i + 1 < n:
            copy(i + 1).start()
        copy(i).wait()
        dst_ref[pl.ds(i * rows, rows), :] = stage_ref[i % 2].astype(BF16)


def _ffn_kernel(x_ref, gpre_ref, wg_hbm, wu_hbm, wd_hbm, gpost_ref, o_ref,
                wg_ref, wu_ref, wd_ref, stage_in_ref, stage_out_ref, sem):
    @pl.when(pl.program_id(0) == 0)
    def _():
        _stage_weight(wg_hbm, wg_ref, stage_in_ref, sem)
        _stage_weight(wu_hbm, wu_ref, stage_in_ref, sem)
        _stage_weight(wd_hbm, wd_ref, stage_out_ref, sem)

    halves = [slice(r0, r0 + FFN_PART) for r0 in range(0, FFN_ROWS, FFN_PART)]
    xn = [_rms(x_ref[rows, :], gpre_ref[...]).astype(BF16) for rows in halves]
    act = []
    for xh in xn:
        g = _mm(xh, wg_ref[...])
        u = _mm(xh, wu_ref[...])
        act.append((g * jax.nn.sigmoid(g) * u).astype(BF16))
    f = [_mm(a, wd_ref[...]) for a in act]
    for rows, fh in zip(halves, f):
        o_ref[rows, :] = x_ref[rows, :] + 0.5 * _rms(fh, gpost_ref[...])


def _ffn(x, gpre, wg, wu, wd, gpost):
    t, d = x.shape
    dff = wg.shape[1]
    return pl.pallas_call(
        _ffn_kernel,
        grid=(t // FFN_ROWS,),
        in_specs=[
            _row_spec(FFN_ROWS, d),
            _const_spec((1, d)),
            pl.BlockSpec(memory_space=pl.ANY),
            pl.BlockSpec(memory_space=pl.ANY),
            pl.BlockSpec(memory_space=pl.ANY),
            _const_spec((1, d)),
        ],
        out_specs=_row_spec(FFN_ROWS, d),
        out_shape=jax.ShapeDtypeStruct((t, d), F32),
        scratch_shapes=[
            pltpu.VMEM((d, dff), BF16),
            pltpu.VMEM((d, dff), BF16),
            pltpu.VMEM((dff, d), BF16),
            pltpu.VMEM((2, d // FFN_STAGE_CHUNKS, dff), F32),
            pltpu.VMEM((2, dff // FFN_STAGE_CHUNKS, d), F32),
            pltpu.SemaphoreType.DMA((2,)),
        ],
        compiler_params=_params(),
        name="ffn",
    )(x, gpre, wg, wu, wd, gpost)


def _proj_kernel(tiles_per_seq,
                 h_ref, g_ref, win_ref, glub_ref, gateb_ref, dww_ref, dwb_ref, lng_ref, lnb_ref,
                 kvq_ref, ag_ref, w_ref, cbuf_ref, shift_ref, conv_ref):
    tm = h_ref.shape[0]
    k_ref = kvq_ref.at[:, 0:D_ATTN]
    v_ref = kvq_ref.at[:, D_ATTN:2 * D_ATTN]
    q_ref = kvq_ref.at[:, 2 * D_ATTN:3 * D_ATTN]
    act_ref = ag_ref.at[:, 0:D_CONV]
    gate_ref = ag_ref.at[:, D_CONV:]

    @pl.when(pl.program_id(0) == 0)
    def _():
        w_ref[...] = win_ref[...].astype(BF16)

    u = _rms(h_ref[...], g_ref[...]).astype(BF16)

    def seg(lo, hi):
        return _mm(u, w_ref[:, lo:hi])

    @pl.when((pl.program_id(0) % tiles_per_seq) == 0)
    def _():
        cbuf_ref[0:CONV_HALO, :] = jnp.zeros((CONV_HALO, D_CONV), F32)

    a = D_ATTN
    c0 = 3 * a
    ca = seg(c0, c0 + D_CONV) + glub_ref[:, :D_CONV]
    cb = seg(c0 + D_CONV, c0 + 2 * D_CONV) + glub_ref[:, D_CONV:]
    cbuf_ref[CONV_HALO:CONV_HALO + tm, :] = ca * jax.nn.sigmoid(cb)
    g0 = c0 + 2 * D_CONV
    ngate = gate_ref.shape[1]

    def after(x):
        x = x[0:1, 0:LANES]
        return jnp.where(x != x, x, 0.0)

    def q_tile(lo):
        r = seg(lo, lo + MXU_COLS)
        q_ref[:, lo:lo + MXU_COLS] = (r * (HEAD_DIM ** -0.5 * LOG2E)).astype(BF16)
        return r

    def kv_tile(ref, base, lo):
        r = seg(base + lo, base + lo + MXU_COLS)
        ref[:, lo:lo + MXU_COLS] = r.astype(BF16)
        return r

    def gate_tile(lo):
        r = seg(g0 + lo, g0 + lo + MXU_COLS)
        gate_ref[:, lo:lo + MXU_COLS] = (r + gateb_ref[:, lo:lo + MXU_COLS]).astype(BF16)
        return r

    tiles = ([functools.partial(q_tile, lo) for lo in range(0, a, MXU_COLS)]
             + [functools.partial(kv_tile, k_ref, a, lo) for lo in range(0, a, MXU_COLS)]
             + [functools.partial(kv_tile, v_ref, 2 * a, lo) for lo in range(0, a, MXU_COLS)]
             + [functools.partial(gate_tile, lo) for lo in range(0, ngate, MXU_COLS)])

    base = CONV_HALO - (CONV_WIDTH - 1)
    span = CONV_HALO + tm - SUBLANES
    n_blocks = (D_CONV // LANES) * (tm // CONV_ROWS)
    anchor = 0.0
    done = []
    anchored = 0
    blk = 0
    for l0 in range(0, D_CONV, LANES):
        cs = slice(l0, l0 + LANES)
        for sh in range(1, SUBLANES):
            shift_ref[sh, 0:span, :] = cbuf_ref[pl.ds(sh, span), cs]
        for r0 in range(0, tm, CONV_ROWS):
            while len(done) * n_blocks < (blk + 1) * len(tiles):
                done.append(tiles[len(done)]())
            while anchored < len(done) - CONV_TILE_LAG:
                anchor = anchor + after(done[anchored])
                anchored += 1
            acc = jnp.zeros((CONV_ROWS, LANES), F32) + dwb_ref[:, cs]
            for j in range(CONV_WIDTH):
                al, sh = divmod(base + j, SUBLANES)
                if sh == 0:
                    tap = cbuf_ref[pl.ds(r0 + al * SUBLANES, CONV_ROWS), cs]
                else:
                    tap = shift_ref[sh, pl.ds(r0 + al * SUBLANES, CONV_ROWS), :]
                acc = acc + tap * (dww_ref[j:j + 1, cs] + anchor)
            conv_ref[r0:r0 + CONV_ROWS, cs] = acc
            anchor = after(acc)
            blk += 1
    for tile in tiles[len(done):]:
        tile()
    cbuf_ref[0:CONV_HALO, :] = cbuf_ref[tm:tm + CONV_HALO, :]
    cv = conv_ref[...]
    mu = jnp.mean(cv, axis=-1, keepdims=True)
    xc = cv - mu
    var = jnp.mean(xc * xc, axis=-1, keepdims=True)
    ln = xc * lax.rsqrt(var + EPS) * lng_ref[...] + lnb_ref[...]
    act_ref[...] = (ln * jax.nn.sigmoid(ln)).astype(BF16)


def _proj(h, g, win, glub, gateb, dww, dwb, lng, lnb, seq_len):
    t, d = h.shape
    tm = ROW_TILE
    n = t // tm
    din = win.shape[1]
    ngate = gateb.shape[1]
    outs = [jax.ShapeDtypeStruct((t, 3 * D_ATTN), BF16),
            jax.ShapeDtypeStruct((t, D_CONV + ngate), BF16)]
    return pl.pallas_call(
        functools.partial(_proj_kernel, seq_len // tm),
        grid=(n,),
        in_specs=[
            _row_spec(tm, d),
            _const_spec((1, d)),
            _const_spec((d, din)),
            _const_spec((1, 2 * D_CONV)),
            _const_spec((1, ngate)),
            _const_spec(dww.shape),
            _const_spec(dwb.shape),
            _const_spec(lng.shape),
            _const_spec(lnb.shape),
        ],
        out_specs=[_row_spec(tm, 3 * D_ATTN), _row_spec(tm, D_CONV + ngate)],
        out_shape=outs,
        scratch_shapes=[
            pltpu.VMEM((d, din), BF16),
            pltpu.VMEM((CONV_HALO + tm, D_CONV), F32),
            pltpu.VMEM((SUBLANES, CONV_HALO + tm, LANES), F32),
            pltpu.VMEM((tm, D_CONV), F32),
        ],
        compiler_params=_params(),
        name="proj",
    )(h, g, win, glub, gateb, dww, dwb, lng, lnb)


def _mixer_kernel(tiles_per_seq,
                  kvq_ref, kvp_ref, ag_ref, h_ref,
                  brow_ref, wa32_ref, wc32_ref, wo32_ref, gpost_ref,
                  o_ref, bias_ref, s_ref, attn_ref, wa_ref, wc_ref, wo_ref):
    tm, d = h_ref.shape
    kc_ref = kvq_ref.at[:, 0:D_ATTN]
    vc_ref = kvq_ref.at[:, D_ATTN:2 * D_ATTN]
    q_ref = kvq_ref.at[:, 2 * D_ATTN:3 * D_ATTN]
    kp_ref = kvp_ref.at[:, 0:D_ATTN]
    vp_ref = kvp_ref.at[:, D_ATTN:2 * D_ATTN]
    act_ref = ag_ref.at[:, 0:D_CONV]
    gate_ref = ag_ref.at[:, D_CONV:]
    first = (pl.program_id(0) % tiles_per_seq) == 0

    @pl.when(pl.program_id(0) == 0)
    def _():
        wa_ref[...] = wa32_ref[...].astype(BF16)
        wc_ref[...] = wc32_ref[...].astype(BF16)
        wo_ref[...] = wo32_ref[...].astype(BF16)
        r = lax.broadcasted_iota(jnp.int32, (Q_BLOCK, K_BLOCK), 0)
        k = lax.broadcasted_iota(jnp.int32, (Q_BLOCK, K_BLOCK), 1)
        rel = k - r + (r & (CHUNK - 1))
        in_band = jnp.logical_and(rel >= 0, rel < (LEFT_CHUNKS + 1) * CHUNK)
        for hd in range(N_HEADS):
            rows = jnp.broadcast_to(brow_ref[hd:hd + 1, :], (Q_BLOCK, BIAS_ROW_LEN))
            toep = pltpu.roll(rows, 0, 1, stride=1, stride_axis=0)
            far = brow_ref[hd:hd + 1, 0:1]
            bias_ref[hd] = jnp.where(in_band, toep[:, Q_BLOCK:] - far, NEG_INF).T

    lane = lax.broadcasted_iota(jnp.int32, (1, LANES), 1)
    lo_half = lane < HEAD_DIM
    krow = lax.broadcasted_iota(jnp.int32, (2 * tm, LANES), 0)
    klane = lax.broadcasted_iota(jnp.int32, (2 * tm, LANES), 1)
    kneg = jnp.where(jnp.logical_and(first, krow < tm), NEG_INF, 0.0)
    n_pairs = N_HEADS // 2
    per_pair = 2 * (tm // Q_BLOCK)

    def scores(hp):
        sl = slice(LANES * hp, LANES * (hp + 1))
        k2 = jnp.concatenate([kp_ref[:, sl], kc_ref[:, sl]], axis=0)
        q2 = q_ref[:, sl]
        for hh in range(2):
            mine = lo_half if hh == 0 else jnp.logical_not(lo_half)
            spare = HEAD_DIM if hh == 0 else 0
            other = jnp.where(klane == spare, kneg, 0.0).astype(BF16)
            kz = jnp.where(mine, k2, other)
            qz = jnp.where(mine, q2, jnp.ones_like(q2))
            for bi, r0 in enumerate(range(0, tm, Q_BLOCK)):
                s_ref[(hp % SCORE_SLOTS) * per_pair + hh * (tm // Q_BLOCK) + bi] = lax.dot_general(
                    kz[r0:r0 + K_BLOCK], qz[r0:r0 + Q_BLOCK],
                    (((1,), (1,)), ((), ())), preferred_element_type=F32)

    for hp in range(SCORE_SLOTS - 1):
        scores(hp)
    for hp in range(n_pairs):
        if hp + SCORE_SLOTS - 1 < n_pairs:
            scores(hp + SCORE_SLOTS - 1)
        sl = slice(LANES * hp, LANES * (hp + 1))
        v2 = jnp.concatenate([vp_ref[:, sl], vc_ref[:, sl]], axis=0)
        for hh in range(2):
            hd = 2 * hp + hh
            spare = HEAD_DIM if hh == 0 else 0
            vz = jnp.where(klane == spare, jnp.ones_like(v2), v2)
            for bi, r0 in enumerate(range(0, tm, Q_BLOCK)):
                st = s_ref.at[(hp % SCORE_SLOTS) * per_pair + hh * (tm // Q_BLOCK) + bi]
                cols = []
                for q0 in range(0, Q_BLOCK, LANES):
                    qs = slice(q0, q0 + LANES)
                    lo_rows = slice(q0, q0 + BAND_EDGE)
                    mid_rows = slice(q0 + BAND_EDGE, q0 + FAR_KEYS)
                    hi_rows = slice(q0 + FAR_KEYS, q0 + SOFTMAX_KEYS)
                    sc = jnp.concatenate([st[lo_rows, qs] + bias_ref[hd, lo_rows, qs],
                                          st[mid_rows, qs],
                                          st[hi_rows, qs] + bias_ref[hd, hi_rows, qs]], axis=0)
                    m = jnp.max(sc, axis=0, keepdims=True)
                    p = jnp.exp2(sc - m)
                    pieces = [p.astype(BF16)]
                    if q0 > 0:
                        pieces.insert(0, jnp.zeros((q0, LANES), BF16))
                    if q0 + SOFTMAX_KEYS < K_BLOCK:
                        pieces.append(jnp.zeros((K_BLOCK - q0 - SOFTMAX_KEYS, LANES), BF16))
                    cols.append(jnp.concatenate(pieces, axis=0))
                pt = jnp.concatenate(cols, axis=1)
                ot = lax.dot_general(vz[r0:r0 + K_BLOCK], pt, (((0,), (0,)), ((), ())),
                                     preferred_element_type=F32)
                ot = ot[hh * HEAD_DIM:(hh + 1) * HEAD_DIM] / ot[spare:spare + 1]
                attn_ref[hd * HEAD_DIM:(hd + 1) * HEAD_DIM, r0:r0 + Q_BLOCK] = ot.astype(BF16)
    y_a = lax.dot_general(attn_ref[...], wa_ref[...], (((0,), (0,)), ((), ())),
                          preferred_element_type=F32)
    y_b = _mm(act_ref[...], wc_ref[...])

    gate_a = jax.nn.sigmoid(gate_ref[:, :d].astype(F32))
    gate_b = jax.nn.sigmoid(gate_ref[:, d:].astype(F32))
    merged = (gate_a * y_a + gate_b * y_b).astype(BF16)
    out = _mm(merged, wo_ref[...])
    o_ref[...] = h_ref[...] + _rms(out, gpost_ref[...])


def _mixer(kvq, act_gates, h, brow, wa, wc, wo, gpost, seq_len):
    t, d = h.shape
    tm = ROW_TILE
    tiles_per_seq = seq_len // tm
    prev = lambda i: (jnp.maximum(i - 1, 0), 0)
    return pl.pallas_call(
        functools.partial(_mixer_kernel, tiles_per_seq),
        grid=(t // tm,),
        in_specs=[
            _row_spec(tm, 3 * D_ATTN),
            pl.BlockSpec((tm, 2 * D_ATTN), prev),
            _row_spec(tm, act_gates.shape[1]),
            _row_spec(tm, d),
            _const_spec(brow.shape),
            _const_spec(wa.shape),
            _const_spec(wc.shape),
            _const_spec(wo.shape),
            _const_spec(gpost.shape),
        ],
        out_specs=_row_spec(tm, d),
        out_shape=jax.ShapeDtypeStruct((t, d), F32),
        scratch_shapes=[
            pltpu.VMEM((N_HEADS, K_BLOCK, Q_BLOCK), F32),
            pltpu.VMEM((SCORE_SLOTS * 2 * tm // Q_BLOCK, K_BLOCK, Q_BLOCK), F32),
            pltpu.VMEM((D_ATTN, tm), BF16),
            pltpu.VMEM(wa.shape, BF16),
            pltpu.VMEM(wc.shape, BF16),
            pltpu.VMEM(wo.shape, BF16),
        ],
        compiler_params=_params(),
        name="mixer",
    )(kvq, kvq, act_gates, h, brow, wa, wc, wo, gpost)


def _rel_bias_rows(rel_table):
    look_back = LEFT_CHUNKS * CHUNK
    n_far = look_back + Q_BLOCK - REL_CLIP
    n_near = BIAS_ROW_LEN - n_far - (2 * REL_CLIP + 1)
    t = rel_table.astype(F32)
    hd = t.shape[0]
    row = jnp.concatenate([jnp.broadcast_to(t[:, -1:], (hd, n_far)), t[:, ::-1],
                           jnp.broadcast_to(t[:, :1], (hd, n_near))], axis=1)
    return row * LOG2E


def kernel(x, ffn1_norm_pre, ffn1_w_gate, ffn1_w_up, ffn1_w_down, ffn1_norm_post, mix_norm_pre, w_in, gate_bias, rel_table, w_attn_out, conv_glu_bias, conv_dw_w, conv_dw_b, conv_ln_g, conv_ln_b, conv_w_out, w_out, mix_norm_post, ffn2_norm_pre, ffn2_w_gate, ffn2_w_up, ffn2_w_down, ffn2_norm_post):
    b, s, d = x.shape
    assert s % ROW_TILE == 0 and d % LANES == 0
    h = x.reshape(b * s, d)
    depth = w_in.shape[0]
    row = lambda p: p.reshape(1, -1)
    for l in range(depth):
        h = _ffn(h, row(ffn1_norm_pre[l]), ffn1_w_gate[l], ffn1_w_up[l], ffn1_w_down[l],
                 row(ffn1_norm_post[l]))
        kvq, act_gates = _proj(h, row(mix_norm_pre[l]), w_in[l],
                               row(conv_glu_bias[l]), row(gate_bias[l]),
                               conv_dw_w[l].reshape(CONV_WIDTH, D_CONV), row(conv_dw_b[l]),
                               row(conv_ln_g[l]), row(conv_ln_b[l]), s)
        h = _mixer(kvq, act_gates, h, _rel_bias_rows(rel_table[l]), w_attn_out[l],
                   conv_w_out[l], w_out[l], row(mix_norm_post[l]), s)
        h = _ffn(h, row(ffn2_norm_pre[l]), ffn2_w_gate[l], ffn2_w_up[l], ffn2_w_down[l],
                 row(ffn2_norm_post[l]))
    return h.reshape(b, s, d)
```

```python
import functools

import jax
import jax.numpy as jnp
from jax import lax
from jax.experimental import pallas as pl
from jax.experimental.pallas import tpu as pltpu

CHUNK = 64
LEFT_CHUNKS = 8
N_HEADS = 8
HEAD_DIM = 64
D_ATTN = N_HEADS * HEAD_DIM
REL_CLIP = 128
D_CONV = 512
CONV_WIDTH = 31
EPS = 1e-6
NEG_INF = -1e30
LOG2E = 1.4426950408889634

LANES = 128
SUBLANES = 8
MXU_COLS = 256
VMEM_LIMIT_BYTES = 56 * 1024 * 1024

ROW_TILE = LEFT_CHUNKS * CHUNK
FFN_ROWS = 2 * ROW_TILE
FFN_PART = ROW_TILE // 2
FFN_STAGE_CHUNKS = 8
Q_BLOCK = 4 * CHUNK
K_BLOCK = Q_BLOCK + LEFT_CHUNKS * CHUNK
SOFTMAX_KEYS = (LEFT_CHUNKS + 1) * CHUNK + (LANES - CHUNK)
BIAS_ROW_LEN = Q_BLOCK + K_BLOCK
BAND_EDGE = LANES - CHUNK
FAR_KEYS = LEFT_CHUNKS * CHUNK - REL_CLIP
SCORE_SLOTS = 4
CONV_HALO = 32
CONV_ROWS = 64
CONV_TILE_LAG = 3

BF16 = jnp.bfloat16
F32 = jnp.float32


def _rms(x, g):
    ms = jnp.mean(x * x, axis=-1, keepdims=True)
    return x * lax.rsqrt(ms + EPS) * g


def _mm(a, w):
    return lax.dot_general(a, w, (((1,), (0,)), ((), ())), preferred_element_type=F32)


def _params():
    return pltpu.CompilerParams(
        dimension_semantics=("arbitrary",), vmem_limit_bytes=VMEM_LIMIT_BYTES)


def _const_spec(shape):
    nd = len(shape)
    return pl.BlockSpec(shape, lambda i: (0,) * nd, pipeline_mode=pl.Buffered(1))


def _row_spec(rows, cols):
    return pl.BlockSpec((rows, cols), lambda i: (i, 0))


def _stage_weight(src_hbm, dst_ref, stage_ref, sem):
    rows = stage_ref.shape[1]
    n = src_hbm.shape[0] // rows

    def copy(i):
        return pltpu.make_async_copy(
            src_hbm.at[pl.ds(i * rows, rows)], stage_ref.at[i % 2], sem.at[i % 2])

    copy(0).start()
    for i in range(n):
        if i + 1 < n:
            copy(i + 1).start()
        copy(i).wait()
        dst_ref[pl.ds(i * rows, rows), :] = stage_ref[i % 2].astype(BF16)


def _ffn_kernel(x_ref, gpre_ref, wg_hbm, wu_hbm, wd_hbm, gpost_ref, o_ref,
                wg_ref, wu_ref, wd_ref, stage_in_ref, stage_out_ref, sem):
    @pl.when(pl.program_id(0) == 0)
    def _():
        _stage_weight(wg_hbm, wg_ref, stage_in_ref, sem)
        _stage_weight(wu_hbm, wu_ref, stage_in_ref, sem)
        _stage_weight(wd_hbm, wd_ref, stage_out_ref, sem)

    halves = [slice(r0, r0 + FFN_PART) for r0 in range(0, FFN_ROWS, FFN_PART)]
    xn = [_rms(x_ref[rows, :], gpre_ref[...]).astype(BF16) for rows in halves]
    act = []
    for xh in xn:
        g = _mm(xh, wg_ref[...])
        u = _mm(xh, wu_ref[...])
        act.append((g * jax.nn.sigmoid(g) * u).astype(BF16))
    f = [_mm(a, wd_ref[...]) for a in act]
    for rows, fh in zip(halves, f):
        o_ref[rows, :] = x_ref[rows, :] + 0.5 * _rms(fh, gpost_ref[...])


def _ffn(x, gpre, wg, wu, wd, gpost):
    t, d = x.shape
    dff = wg.shape[1]
    return pl.pallas_call(
        _ffn_kernel,
        grid=(t // FFN_ROWS,),
        in_specs=[
            _row_spec(FFN_ROWS, d),
            _const_spec((1, d)),
            pl.BlockSpec(memory_space=pl.ANY),
            pl.BlockSpec(memory_space=pl.ANY),
            pl.BlockSpec(memory_space=pl.ANY),
            _const_spec((1, d)),
        ],
        out_specs=_row_spec(FFN_ROWS, d),
        out_shape=jax.ShapeDtypeStruct((t, d), F32),
        scratch_shapes=[
            pltpu.VMEM((d, dff), BF16),
            pltpu.VMEM((d, dff), BF16),
            pltpu.VMEM((dff, d), BF16),
            pltpu.VMEM((2, d // FFN_STAGE_CHUNKS, dff), F32),
            pltpu.VMEM((2, dff // FFN_STAGE_CHUNKS, d), F32),
            pltpu.SemaphoreType.DMA((2,)),
        ],
        compiler_params=_params(),
        name="ffn",
    )(x, gpre, wg, wu, wd, gpost)


def _proj_kernel(tiles_per_seq,
                 h_ref, g_ref, win_ref, glub_ref, gateb_ref, dww_ref, dwb_ref, lng_ref, lnb_ref,
                 kvq_ref, ag_ref, w_ref, cbuf_ref, shift_ref, conv_ref, wblk_ref):
    tm = h_ref.shape[0]
    k_ref = kvq_ref.at[:, 0:D_ATTN]
    v_ref = kvq_ref.at[:, D_ATTN:2 * D_ATTN]
    q_ref = kvq_ref.at[:, 2 * D_ATTN:3 * D_ATTN]
    act_ref = ag_ref.at[:, 0:D_CONV]
    gate_ref = ag_ref.at[:, D_CONV:]

    @pl.when(pl.program_id(0) == 0)
    def _():
        w_ref[...] = win_ref[...].astype(BF16)

    u = _rms(h_ref[...], g_ref[...]).astype(BF16)

    def seg(lo, hi):
        return _mm(u, w_ref[:, lo:hi])

    @pl.when((pl.program_id(0) % tiles_per_seq) == 0)
    def _():
        cbuf_ref[0:CONV_HALO, :] = jnp.zeros((CONV_HALO, D_CONV), F32)

    a = D_ATTN
    c0 = 3 * a
    ca = seg(c0, c0 + D_CONV) + glub_ref[:, :D_CONV]
    cb = seg(c0 + D_CONV, c0 + 2 * D_CONV) + glub_ref[:, D_CONV:]
    cbuf_ref[CONV_HALO:CONV_HALO + tm, :] = ca * jax.nn.sigmoid(cb)
    g0 = c0 + 2 * D_CONV
    ngate = gate_ref.shape[1]

    def after(x):
        x = x[0:1, 0:LANES]
        return jnp.where(x != x, x, 0.0)

    def q_tile(lo):
        r = seg(lo, lo + MXU_COLS)
        q_ref[:, lo:lo + MXU_COLS] = (r * (HEAD_DIM ** -0.5 * LOG2E)).astype(BF16)
        return r

    def kv_tile(ref, base, lo):
        r = seg(base + lo, base + lo + MXU_COLS)
        ref[:, lo:lo + MXU_COLS] = r.astype(BF16)
        return r

    def gate_tile(lo):
        r = seg(g0 + lo, g0 + lo + MXU_COLS)
        gate_ref[:, lo:lo + MXU_COLS] = (r + gateb_ref[:, lo:lo + MXU_COLS]).astype(BF16)
        return r

    tiles = ([functools.partial(q_tile, lo) for lo in range(0, a, MXU_COLS)]
             + [functools.partial(kv_tile, k_ref, a, lo) for lo in range(0, a, MXU_COLS)]
             + [functools.partial(kv_tile, v_ref, 2 * a, lo) for lo in range(0, a, MXU_COLS)]
             + [functools.partial(gate_tile, lo) for lo in range(0, ngate, MXU_COLS)])

    base = CONV_HALO - (CONV_WIDTH - 1)
    span = CONV_HALO + tm - SUBLANES
    n_blocks = (D_CONV // LANES) * (tm // CONV_ROWS)
    anchor = 0.0
    done = []
    anchored = 0
    blk = 0
    for l0 in range(0, D_CONV, LANES):
        cs = slice(l0, l0 + LANES)
        for sh in range(1, SUBLANES):
            shift_ref[sh, 0:span, :] = cbuf_ref[pl.ds(sh, span), cs]
        for r0 in range(0, tm, CONV_ROWS):
            while len(done) * n_blocks < (blk + 1) * len(tiles):
                done.append(tiles[len(done)]())
            while anchored < len(done) - CONV_TILE_LAG:
                anchor = anchor + after(done[anchored])
                anchored += 1
            wblk_ref[0:CONV_WIDTH, :] = dww_ref[:, cs] + anchor
            acc = jnp.zeros((CONV_ROWS, LANES), F32) + dwb_ref[:, cs]
            for j in range(CONV_WIDTH):
                al, sh = divmod(base + j, SUBLANES)
                if sh == 0:
                    tap = cbuf_ref[pl.ds(r0 + al * SUBLANES, CONV_ROWS), cs]
                else:
                    tap = shift_ref[sh, pl.ds(r0 + al * SUBLANES, CONV_ROWS), :]
                acc = acc + tap * wblk_ref[j:j + 1, :]
            conv_ref[r0:r0 + CONV_ROWS, cs] = acc
            anchor = after(acc)
            blk += 1
    for tile in tiles[len(done):]:
        tile()
    cbuf_ref[0:CONV_HALO, :] = cbuf_ref[tm:tm + CONV_HALO, :]
    cv = conv_ref[...]
    mu = jnp.mean(cv, axis=-1, keepdims=True)
    xc = cv - mu
    var = jnp.mean(xc * xc, axis=-1, keepdims=True)
    ln = xc * lax.rsqrt(var + EPS) * lng_ref[...] + lnb_ref[...]
    act_ref[...] = (ln * jax.nn.sigmoid(ln)).astype(BF16)


def _proj(h, g, win, glub, gateb, dww, dwb, lng, lnb, seq_len):
    t, d = h.shape
    tm = ROW_TILE
    n = t // tm
    din = win.shape[1]
    ngate = gateb.shape[1]
    outs = [jax.ShapeDtypeStruct((t, 3 * D_ATTN), BF16),
            jax.ShapeDtypeStruct((t, D_CONV + ngate), BF16)]
    return pl.pallas_call(
        functools.partial(_proj_kernel, seq_len // tm),
        grid=(n,),
        in_specs=[
            _row_spec(tm, d),
            _const_spec((1, d)),
            _const_spec((d, din)),
            _const_spec((1, 2 * D_CONV)),
            _const_spec((1, ngate)),
            _const_spec(dww.shape),
            _const_spec(dwb.shape),
            _const_spec(lng.shape),
            _const_spec(lnb.shape),
        ],
        out_specs=[_row_spec(tm, 3 * D_ATTN), _row_spec(tm, D_CONV + ngate)],
        out_shape=outs,
        scratch_shapes=[
            pltpu.VMEM((d, din), BF16),
            pltpu.VMEM((CONV_HALO + tm, D_CONV), F32),
            pltpu.VMEM((SUBLANES, CONV_HALO + tm, LANES), F32),
            pltpu.VMEM((tm, D_CONV), F32),
            pltpu.VMEM((pl.cdiv(CONV_WIDTH, SUBLANES) * SUBLANES, LANES), F32),
        ],
        compiler_params=_params(),
        name="proj",
    )(h, g, win, glub, gateb, dww, dwb, lng, lnb)


def _mixer_kernel(tiles_per_seq,
                  kvq_ref, kvp_ref, ag_ref, h_ref,
                  brow_ref, wa32_ref, wc32_ref, wo32_ref, gpost_ref,
                  o_ref, bias_ref, s_ref, attn_ref, wa_ref, wc_ref, wo_ref):
    tm, d = h_ref.shape
    kc_ref = kvq_ref.at[:, 0:D_ATTN]
    vc_ref = kvq_ref.at[:, D_ATTN:2 * D_ATTN]
    q_ref = kvq_ref.at[:, 2 * D_ATTN:3 * D_ATTN]
    kp_ref = kvp_ref.at[:, 0:D_ATTN]
    vp_ref = kvp_ref.at[:, D_ATTN:2 * D_ATTN]
    act_ref = ag_ref.at[:, 0:D_CONV]
    gate_ref = ag_ref.at[:, D_CONV:]
    first = (pl.program_id(0) % tiles_per_seq) == 0

    @pl.when(pl.program_id(0) == 0)
    def _():
        wa_ref[...] = wa32_ref[...].astype(BF16)
        wc_ref[...] = wc32_ref[...].astype(BF16)
        wo_ref[...] = wo32_ref[...].astype(BF16)
        r = lax.broadcasted_iota(jnp.int32, (Q_BLOCK, K_BLOCK), 0)
        k = lax.broadcasted_iota(jnp.int32, (Q_BLOCK, K_BLOCK), 1)
        rel = k - r + (r & (CHUNK - 1))
        in_band = jnp.logical_and(rel >= 0, rel < (LEFT_CHUNKS + 1) * CHUNK)
        for hd in range(N_HEADS):
            rows = jnp.broadcast_to(brow_ref[hd:hd + 1, :], (Q_BLOCK, BIAS_ROW_LEN))
            toep = pltpu.roll(rows, 0, 1, stride=1, stride_axis=0)
            far = brow_ref[hd:hd + 1, 0:1]
            bias_ref[hd] = jnp.where(in_band, toep[:, Q_BLOCK:] - far, NEG_INF).T

    lane = lax.broadcasted_iota(jnp.int32, (1, LANES), 1)
    lo_half = lane < HEAD_DIM
    krow = lax.broadcasted_iota(jnp.int32, (2 * tm, LANES), 0)
    klane = lax.broadcasted_iota(jnp.int32, (2 * tm, LANES), 1)
    kneg = jnp.where(jnp.logical_and(first, krow < tm), NEG_INF, 0.0)
    n_pairs = N_HEADS // 2
    per_pair = 2 * (tm // Q_BLOCK)

    def scores(hp):
        sl = slice(LANES * hp, LANES * (hp + 1))
        k2 = jnp.concatenate([kp_ref[:, sl], kc_ref[:, sl]], axis=0)
        q2 = q_ref[:, sl]
        for hh in range(2):
            mine = lo_half if hh == 0 else jnp.logical_not(lo_half)
            spare = HEAD_DIM if hh == 0 else 0
            other = jnp.where(klane == spare, kneg, 0.0).astype(BF16)
            kz = jnp.where(mine, k2, other)
            qz = jnp.where(mine, q2, jnp.ones_like(q2))
            for bi, r0 in enumerate(range(0, tm, Q_BLOCK)):
                s_ref[(hp % SCORE_SLOTS) * per_pair + hh * (tm // Q_BLOCK) + bi] = lax.dot_general(
                    kz[r0:r0 + K_BLOCK], qz[r0:r0 + Q_BLOCK],
                    (((1,), (1,)), ((), ())), preferred_element_type=F32)

    for hp in range(SCORE_SLOTS - 1):
        scores(hp)
    for hp in range(n_pairs):
        if hp + SCORE_SLOTS - 1 < n_pairs:
            scores(hp + SCORE_SLOTS - 1)
        sl = slice(LANES * hp, LANES * (hp + 1))
        v2 = jnp.concatenate([vp_ref[:, sl], vc_ref[:, sl]], axis=0)
        for hh in range(2):
            hd = 2 * hp + hh
            spare = HEAD_DIM if hh == 0 else 0
            vz = jnp.where(klane == spare, jnp.ones_like(v2), v2)
            for bi, r0 in enumerate(range(0, tm, Q_BLOCK)):
                st = s_ref.at[(hp % SCORE_SLOTS) * per_pair + hh * (tm // Q_BLOCK) + bi]
                cols = []
                for q0 in range(0, Q_BLOCK, LANES):
                    qs = slice(q0, q0 + LANES)
                    lo_rows = slice(q0, q0 + BAND_EDGE)
                    mid_rows = slice(q0 + BAND_EDGE, q0 + FAR_KEYS)
                    hi_rows = slice(q0 + FAR_KEYS, q0 + SOFTMAX_KEYS)
                    sc = jnp.concatenate([st[lo_rows, qs] + bias_ref[hd, lo_rows, qs],
                                          st[mid_rows, qs],
                                          st[hi_rows, qs] + bias_ref[hd, hi_rows, qs]], axis=0)
                    m = jnp.max(sc, axis=0, keepdims=True)
                    p = jnp.exp2(sc - m)
                    pieces = [p.astype(BF16)]
                    if q0 > 0:
                        pieces.insert(0, jnp.zeros((q0, LANES), BF16))
                    if q0 + SOFTMAX_KEYS < K_BLOCK:
                        pieces.append(jnp.zeros((K_BLOCK - q0 - SOFTMAX_KEYS, LANES), BF16))
                    cols.append(jnp.concatenate(pieces, axis=0))
                pt = jnp.concatenate(cols, axis=1)
                ot = lax.dot_general(vz[r0:r0 + K_BLOCK], pt, (((0,), (0,)), ((), ())),
                                     preferred_element_type=F32)
                ot = ot[hh * HEAD_DIM:(hh + 1) * HEAD_DIM] / ot[spare:spare + 1]
                attn_ref[hd * HEAD_DIM:(hd + 1) * HEAD_DIM, r0:r0 + Q_BLOCK] = ot.astype(BF16)
    y_a = lax.dot_general(attn_ref[...], wa_ref[...], (((0,), (0,)), ((), ())),
                          preferred_element_type=F32)
    y_b = _mm(act_ref[...], wc_ref[...])

    gate_a = jax.nn.sigmoid(gate_ref[:, :d].astype(F32))
    gate_b = jax.nn.sigmoid(gate_ref[:, d:].astype(F32))
    merged = (gate_a * y_a + gate_b * y_b).astype(BF16)
    out = _mm(merged, wo_ref[...])
    o_ref[...] = h_ref[...] + _rms(out, gpost_ref[...])


def _mixer(kvq, act_gates, h, brow, wa, wc, wo, gpost, seq_len):
    t, d = h.shape
    tm = ROW_TILE
    tiles_per_seq = seq_len // tm
    prev = lambda i: (jnp.maximum(i - 1, 0), 0)
    return pl.pallas_call(
        functools.partial(_mixer_kernel, tiles_per_seq),
        grid=(t // tm,),
        in_specs=[
            _row_spec(tm, 3 * D_ATTN),
            pl.BlockSpec((tm, 2 * D_ATTN), prev),
            _row_spec(tm, act_gates.shape[1]),
            _row_spec(tm, d),
            _const_spec(brow.shape),
            _const_spec(wa.shape),
            _const_spec(wc.shape),
            _const_spec(wo.shape),
            _const_spec(gpost.shape),
        ],
        out_specs=_row_spec(tm, d),
        out_shape=jax.ShapeDtypeStruct((t, d), F32),
        scratch_shapes=[
            pltpu.VMEM((N_HEADS, K_BLOCK, Q_BLOCK), F32),
            pltpu.VMEM((SCORE_SLOTS * 2 * tm // Q_BLOCK, K_BLOCK, Q_BLOCK), F32),
            pltpu.VMEM((D_ATTN, tm), BF16),
            pltpu.VMEM(wa.shape, BF16),
            pltpu.VMEM(wc.shape, BF16),
            pltpu.VMEM(wo.shape, BF16),
        ],
        compiler_params=_params(),
        name="mixer",
    )(kvq, kvq, act_gates, h, brow, wa, wc, wo, gpost)


def _rel_bias_rows(rel_table):
    look_back = LEFT_CHUNKS * CHUNK
    n_far = look_back + Q_BLOCK - REL_CLIP
    n_near = BIAS_ROW_LEN - n_far - (2 * REL_CLIP + 1)
    t = rel_table.astype(F32)
    hd = t.shape[0]
    row = jnp.concatenate([jnp.broadcast_to(t[:, -1:], (hd, n_far)), t[:, ::-1],
                           jnp.broadcast_to(t[:, :1], (hd, n_near))], axis=1)
    return row * LOG2E


def kernel(x, ffn1_norm_pre, ffn1_w_gate, ffn1_w_up, ffn1_w_down, ffn1_norm_post, mix_norm_pre, w_in, gate_bias, rel_table, w_attn_out, conv_glu_bias, conv_dw_w, conv_dw_b, conv_ln_g, conv_ln_b, conv_w_out, w_out, mix_norm_post, ffn2_norm_pre, ffn2_w_gate, ffn2_w_up, ffn2_w_down, ffn2_norm_post):
    b, s, d = x.shape
    assert s % ROW_TILE == 0 and d % LANES == 0
    h = x.reshape(b * s, d)
    depth = w_in.shape[0]
    row = lambda p: p.reshape(1, -1)
    for l in range(depth):
        h = _ffn(h, row(ffn1_norm_pre[l]), ffn1_w_gate[l], ffn1_w_up[l], ffn1_w_down[l],
                 row(ffn1_norm_post[l]))
        kvq, act_gates = _proj(h, row(mix_norm_pre[l]), w_in[l],
                               row(conv_glu_bias[l]), row(gate_bias[l]),
                               conv_dw_w[l].reshape(CONV_WIDTH, D_CONV), row(conv_dw_b[l]),
                               row(conv_ln_g[l]), row(conv_ln_b[l]), s)
        h = _mixer(kvq, act_gates, h, _rel_bias_rows(rel_table[l]), w_attn_out[l],
                   conv_w_out[l], w_out[l], row(mix_norm_post[l]), s)
        h = _ffn(h, row(ffn2_norm_pre[l]), ffn2_w_gate[l], ffn2_w_up[l], ffn2_w_down[l],
                 row(ffn2_norm_post[l]))
    return h.reshape(b, s, d)
```

```python
import functools

import jax
import jax.numpy as jnp
from jax import lax
from jax.experimental import pallas as pl
from jax.experimental.pallas import tpu as pltpu

CHUNK = 64
LEFT_CHUNKS = 8
N_HEADS = 8
HEAD_DIM = 64
D_ATTN = N_HEADS * HEAD_DIM
REL_CLIP = 128
D_CONV = 512
CONV_WIDTH = 31
EPS = 1e-6
NEG_INF = -1e30
LOG2E = 1.4426950408889634

LANES = 128
SUBLANES = 8
MXU_COLS = 256
VMEM_LIMIT_BYTES = 56 * 1024 * 1024

ROW_TILE = LEFT_CHUNKS * CHUNK
FFN_ROWS = 2 * ROW_TILE
FFN_PART = ROW_TILE // 2
FFN_STAGE_CHUNKS = 8
Q_BLOCK = 4 * CHUNK
K_BLOCK = Q_BLOCK + LEFT_CHUNKS * CHUNK
SOFTMAX_KEYS = (LEFT_CHUNKS + 1) * CHUNK + (LANES - CHUNK)
BIAS_ROW_LEN = Q_BLOCK + K_BLOCK
BAND_EDGE = LANES - CHUNK
FAR_KEYS = LEFT_CHUNKS * CHUNK - REL_CLIP
SCORE_SLOTS = 4
CONV_HALO = 32
CONV_ROWS = 64
CONV_TILE_LAG = 3

BF16 = jnp.bfloat16
F32 = jnp.float32


def _rms(x, g):
    ms = jnp.mean(x * x, axis=-1, keepdims=True)
    return x * lax.rsqrt(ms + EPS) * g


def _mm(a, w):
    return lax.dot_general(a, w, (((1,), (0,)), ((), ())), preferred_element_type=F32)


def _params():
    return pltpu.CompilerParams(
        dimension_semantics=("arbitrary",), vmem_limit_bytes=VMEM_LIMIT_BYTES)


def _const_spec(shape):
    nd = len(shape)
    return pl.BlockSpec(shape, lambda i: (0,) * nd, pipeline_mode=pl.Buffered(1))


def _row_spec(rows, cols):
    return pl.BlockSpec((rows, cols), lambda i: (i, 0))


def _stage_weight(src_hbm, dst_ref, stage_ref, sem):
    rows = stage_ref.shape[1]
    n = src_hbm.shape[0] // rows

    def copy(i):
        return pltpu.make_async_copy(
            src_hbm.at[pl.ds(i * rows, rows)], stage_ref.at[i % 2], sem.at[i % 2])

    copy(0).start()
    for i in range(n):
        if i + 1 < n:
            copy(i + 1).start()
        copy(i).wait()
        dst_ref[pl.ds(i * rows, rows), :] = stage_ref[i % 2].astype(BF16)


def _ffn_kernel(x_ref, gpre_ref, wg_hbm, wu_hbm, wd_hbm, gpost_ref, o_ref,
                wg_ref, wu_ref, wd_ref, stage_in_ref, stage_out_ref, sem):
    @pl.when(pl.program_id(0) == 0)
    def _():
        _stage_weight(wg_hbm, wg_ref, stage_in_ref, sem)
        _stage_weight(wu_hbm, wu_ref, stage_in_ref, sem)
        _stage_weight(wd_hbm, wd_ref, stage_out_ref, sem)

    halves = [slice(r0, r0 + FFN_PART) for r0 in range(0, FFN_ROWS, FFN_PART)]
    xn = [_rms(x_ref[rows, :], gpre_ref[...]).astype(BF16) for rows in halves]
    act = []
    for xh in xn:
        g = _mm(xh, wg_ref[...])
        u = _mm(xh, wu_ref[...])
        act.append((g * jax.nn.sigmoid(g) * u).astype(BF16))
    f = [_mm(a, wd_ref[...]) for a in act]
    for rows, fh in zip(halves, f):
        o_ref[rows, :] = x_ref[rows, :] + 0.5 * _rms(fh, gpost_ref[...])


def _ffn(x, gpre, wg, wu, wd, gpost):
    t, d = x.shape
    dff = wg.shape[1]
    return pl.pallas_call(
        _ffn_kernel,
        grid=(t // FFN_ROWS,),
        in_specs=[
            _row_spec(FFN_ROWS, d),
            _const_spec((1, d)),
            pl.BlockSpec(memory_space=pl.ANY),
            pl.BlockSpec(memory_space=pl.ANY),
            pl.BlockSpec(memory_space=pl.ANY),
            _const_spec((1, d)),
        ],
        out_specs=_row_spec(FFN_ROWS, d),
        out_shape=jax.ShapeDtypeStruct((t, d), F32),
        scratch_shapes=[
            pltpu.VMEM((d, dff), BF16),
            pltpu.VMEM((d, dff), BF16),
            pltpu.VMEM((dff, d), BF16),
            pltpu.VMEM((2, d // FFN_STAGE_CHUNKS, dff), F32),
            pltpu.VMEM((2, dff // FFN_STAGE_CHUNKS, d), F32),
            pltpu.SemaphoreType.DMA((2,)),
        ],
        compiler_params=_params(),
        name="ffn",
    )(x, gpre, wg, wu, wd, gpost)


def _proj_kernel(tiles_per_seq,
                 h_ref, g_ref, win_ref, glub_ref, gateb_ref, dww_ref, dwb_ref, lng_ref, lnb_ref,
                 kvq_ref, ag_ref, w_ref, cbuf_ref, shift_ref, conv_ref, wblk_ref):
    tm = h_ref.shape[0]
    k_ref = kvq_ref.at[:, 0:D_ATTN]
    v_ref = kvq_ref.at[:, D_ATTN:2 * D_ATTN]
    q_ref = kvq_ref.at[:, 2 * D_ATTN:3 * D_ATTN]
    act_ref = ag_ref.at[:, 0:D_CONV]
    gate_ref = ag_ref.at[:, D_CONV:]

    @pl.when(pl.program_id(0) == 0)
    def _():
        w_ref[...] = win_ref[...].astype(BF16)

    u = _rms(h_ref[...], g_ref[...]).astype(BF16)

    def seg(lo, hi):
        return _mm(u, w_ref[:, lo:hi])

    @pl.when((pl.program_id(0) % tiles_per_seq) == 0)
    def _():
        cbuf_ref[0:CONV_HALO, :] = jnp.zeros((CONV_HALO, D_CONV), F32)

    a = D_ATTN
    c0 = 3 * a
    ca = seg(c0, c0 + D_CONV) + glub_ref[:, :D_CONV]
    cb = seg(c0 + D_CONV, c0 + 2 * D_CONV) + glub_ref[:, D_CONV:]
    cbuf_ref[CONV_HALO:CONV_HALO + tm, :] = ca * jax.nn.sigmoid(cb)
    g0 = c0 + 2 * D_CONV
    ngate = gate_ref.shape[1]

    def after(x):
        x = x[0:1, 0:LANES]
        return jnp.where(x != x, x, 0.0)

    def q_tile(lo):
        r = seg(lo, lo + MXU_COLS)
        q_ref[:, lo:lo + MXU_COLS] = (r * (HEAD_DIM ** -0.5 * LOG2E)).astype(BF16)
        return r

    def kv_tile(ref, base, lo):
        r = seg(base + lo, base + lo + MXU_COLS)
        ref[:, lo:lo + MXU_COLS] = r.astype(BF16)
        return r

    def gate_tile(lo):
        r = seg(g0 + lo, g0 + lo + MXU_COLS)
        gate_ref[:, lo:lo + MXU_COLS] = (r + gateb_ref[:, lo:lo + MXU_COLS]).astype(BF16)
        return r

    tiles = ([functools.partial(q_tile, lo) for lo in range(0, a, MXU_COLS)]
             + [functools.partial(kv_tile, k_ref, a, lo) for lo in range(0, a, MXU_COLS)]
             + [functools.partial(kv_tile, v_ref, 2 * a, lo) for lo in range(0, a, MXU_COLS)]
             + [functools.partial(gate_tile, lo) for lo in range(0, ngate, MXU_COLS)])

    base = CONV_HALO - (CONV_WIDTH - 1)
    span = CONV_HALO + tm - SUBLANES
    n_blocks = (D_CONV // LANES) * (tm // CONV_ROWS)
    anchor = 0.0
    done = []
    anchored = 0
    blk = 0
    for l0 in range(0, D_CONV, LANES):
        cs = slice(l0, l0 + LANES)
        for sh in range(1, SUBLANES):
            shift_ref[sh, 0:span, :] = cbuf_ref[pl.ds(sh, span), cs]
        for r0 in range(0, tm, CONV_ROWS):
            while len(done) * n_blocks < (blk + 1) * len(tiles):
                done.append(tiles[len(done)]())
            while anchored < len(done) - CONV_TILE_LAG:
                anchor = anchor + after(done[anchored])
                anchored += 1
            wblk_ref[0:CONV_WIDTH, :] = dww_ref[:, cs] + anchor
            acc = jnp.zeros((CONV_ROWS, LANES), F32) + dwb_ref[:, cs]
            for j in range(CONV_WIDTH):
                al, sh = divmod(base + j, SUBLANES)
                if sh == 0:
                    tap = cbuf_ref[pl.ds(r0 + al * SUBLANES, CONV_ROWS), cs]
                else:
                    tap = shift_ref[sh, pl.ds(r0 + al * SUBLANES, CONV_ROWS), :]
                acc = acc + tap * wblk_ref[j:j + 1, :]
            conv_ref[r0:r0 + CONV_ROWS, cs] = acc
            anchor = after(acc)
            blk += 1
    for tile in tiles[len(done):]:
        tile()
    cbuf_ref[0:CONV_HALO, :] = cbuf_ref[tm:tm + CONV_HALO, :]
    cv = conv_ref[...]
    mu = jnp.mean(cv, axis=-1, keepdims=True)
    xc = cv - mu
    var = jnp.mean(xc * xc, axis=-1, keepdims=True)
    ln = xc * lax.rsqrt(var + EPS) * lng_ref[...] + lnb_ref[...]
    act_ref[...] = (ln * jax.nn.sigmoid(ln)).astype(BF16)


def _proj(h, g, win, glub, gateb, dww, dwb, lng, lnb, seq_len):
    t, d = h.shape
    tm = ROW_TILE
    n = t // tm
    din = win.shape[1]
    ngate = gateb.shape[1]
    outs = [jax.ShapeDtypeStruct((t, 3 * D_ATTN), BF16),
            jax.ShapeDtypeStruct((t, D_CONV + ngate), BF16)]
    return pl.pallas_call(
        functools.partial(_proj_kernel, seq_len // tm),
        grid=(n,),
        in_specs=[
            _row_spec(tm, d),
            _const_spec((1, d)),
            _const_spec((d, din)),
            _const_spec((1, 2 * D_CONV)),
            _const_spec((1, ngate)),
            _const_spec(dww.shape),
            _const_spec(dwb.shape),
            _const_spec(lng.shape),
            _const_spec(lnb.shape),
        ],
        out_specs=[_row_spec(tm, 3 * D_ATTN), _row_spec(tm, D_CONV + ngate)],
        out_shape=outs,
        scratch_shapes=[
            pltpu.VMEM((d, din), BF16),
            pltpu.VMEM((CONV_HALO + tm, D_CONV), F32),
            pltpu.VMEM((SUBLANES, CONV_HALO + tm, LANES), F32),
            pltpu.VMEM((tm, D_CONV), F32),
            pltpu.VMEM((pl.cdiv(CONV_WIDTH, SUBLANES) * SUBLANES, LANES), F32),
        ],
        compiler_params=_params(),
        name="proj",
    )(h, g, win, glub, gateb, dww, dwb, lng, lnb)


def _mixer_kernel(tiles_per_seq,
                  kvq_ref, ag_ref, h_ref,
                  brow_ref, wa32_ref, wc32_ref, wo32_ref, gpost_ref,
                  o_ref, bias_ref, s_ref, attn_ref, wa_ref, wc_ref, wo_ref, kvp_ref):
    tm, d = h_ref.shape
    kc_ref = kvq_ref.at[:, 0:D_ATTN]
    vc_ref = kvq_ref.at[:, D_ATTN:2 * D_ATTN]
    q_ref = kvq_ref.at[:, 2 * D_ATTN:3 * D_ATTN]
    kp_ref = kvp_ref.at[:, 0:D_ATTN]
    vp_ref = kvp_ref.at[:, D_ATTN:2 * D_ATTN]
    act_ref = ag_ref.at[:, 0:D_CONV]
    gate_ref = ag_ref.at[:, D_CONV:]
    first = (pl.program_id(0) % tiles_per_seq) == 0

    @pl.when(first)
    def _():
        kvp_ref[...] = jnp.zeros(kvp_ref.shape, BF16)

    @pl.when(pl.program_id(0) == 0)
    def _():
        wa_ref[...] = wa32_ref[...].astype(BF16)
        wc_ref[...] = wc32_ref[...].astype(BF16)
        wo_ref[...] = wo32_ref[...].astype(BF16)
        r = lax.broadcasted_iota(jnp.int32, (Q_BLOCK, K_BLOCK), 0)
        k = lax.broadcasted_iota(jnp.int32, (Q_BLOCK, K_BLOCK), 1)
        rel = k - r + (r & (CHUNK - 1))
        in_band = jnp.logical_and(rel >= 0, rel < (LEFT_CHUNKS + 1) * CHUNK)
        for hd in range(N_HEADS):
            rows = jnp.broadcast_to(brow_ref[hd:hd + 1, :], (Q_BLOCK, BIAS_ROW_LEN))
            toep = pltpu.roll(rows, 0, 1, stride=1, stride_axis=0)
            far = brow_ref[hd:hd + 1, 0:1]
            bias_ref[hd] = jnp.where(in_band, toep[:, Q_BLOCK:] - far, NEG_INF).T

    lane = lax.broadcasted_iota(jnp.int32, (1, LANES), 1)
    lo_half = lane < HEAD_DIM
    krow = lax.broadcasted_iota(jnp.int32, (2 * tm, LANES), 0)
    klane = lax.broadcasted_iota(jnp.int32, (2 * tm, LANES), 1)
    kneg = jnp.where(jnp.logical_and(first, krow < tm), NEG_INF, 0.0)
    n_pairs = N_HEADS // 2
    per_pair = 2 * (tm // Q_BLOCK)

    def scores(hp):
        sl = slice(LANES * hp, LANES * (hp + 1))
        k2 = jnp.concatenate([kp_ref[:, sl], kc_ref[:, sl]], axis=0)
        q2 = q_ref[:, sl]
        for hh in range(2):
            mine = lo_half if hh == 0 else jnp.logical_not(lo_half)
            spare = HEAD_DIM if hh == 0 else 0
            other = jnp.where(klane == spare, kneg, 0.0).astype(BF16)
            kz = jnp.where(mine, k2, other)
            qz = jnp.where(mine, q2, jnp.ones_like(q2))
            for bi, r0 in enumerate(range(0, tm, Q_BLOCK)):
                s_ref[(hp % SCORE_SLOTS) * per_pair + hh * (tm // Q_BLOCK) + bi] = lax.dot_general(
                    kz[r0:r0 + K_BLOCK], qz[r0:r0 + Q_BLOCK],
                    (((1,), (1,)), ((), ())), preferred_element_type=F32)

    for hp in range(SCORE_SLOTS - 1):
        scores(hp)
    for hp in range(n_pairs):
        if hp + SCORE_SLOTS - 1 < n_pairs:
            scores(hp + SCORE_SLOTS - 1)
        sl = slice(LANES * hp, LANES * (hp + 1))
        v2 = jnp.concatenate([vp_ref[:, sl], vc_ref[:, sl]], axis=0)
        for hh in range(2):
            hd = 2 * hp + hh
            spare = HEAD_DIM if hh == 0 else 0
            vz = jnp.where(klane == spare, jnp.ones_like(v2), v2)
            for bi, r0 in enumerate(range(0, tm, Q_BLOCK)):
                st = s_ref.at[(hp % SCORE_SLOTS) * per_pair + hh * (tm // Q_BLOCK) + bi]
                cols = []
                for q0 in range(0, Q_BLOCK, LANES):
                    qs = slice(q0, q0 + LANES)
                    lo_rows = slice(q0, q0 + BAND_EDGE)
                    mid_rows = slice(q0 + BAND_EDGE, q0 + FAR_KEYS)
                    hi_rows = slice(q0 + FAR_KEYS, q0 + SOFTMAX_KEYS)
                    sc = jnp.concatenate([st[lo_rows, qs] + bias_ref[hd, lo_rows, qs],
                                          st[mid_rows, qs],
                                          st[hi_rows, qs] + bias_ref[hd, hi_rows, qs]], axis=0)
                    m = jnp.max(sc, axis=0, keepdims=True)
                    p = jnp.exp2(sc - m)
                    pieces = [p.astype(BF16)]
                    if q0 > 0:
                        pieces.insert(0, jnp.zeros((q0, LANES), BF16))
                    if q0 + SOFTMAX_KEYS < K_BLOCK:
                        pieces.append(jnp.zeros((K_BLOCK - q0 - SOFTMAX_KEYS, LANES), BF16))
                    cols.append(jnp.concatenate(pieces, axis=0))
                pt = jnp.concatenate(cols, axis=1)
                ot = lax.dot_general(vz[r0:r0 + K_BLOCK], pt, (((0,), (0,)), ((), ())),
                                     preferred_element_type=F32)
                ot = ot[hh * HEAD_DIM:(hh + 1) * HEAD_DIM] / ot[spare:spare + 1]
                attn_ref[hd * HEAD_DIM:(hd + 1) * HEAD_DIM, r0:r0 + Q_BLOCK] = ot.astype(BF16)
    y_a = lax.dot_general(attn_ref[...], wa_ref[...], (((0,), (0,)), ((), ())),
                          preferred_element_type=F32)
    y_b = _mm(act_ref[...], wc_ref[...])

    gate_a = jax.nn.sigmoid(gate_ref[:, :d].astype(F32))
    gate_b = jax.nn.sigmoid(gate_ref[:, d:].astype(F32))
    merged = (gate_a * y_a + gate_b * y_b).astype(BF16)
    out = _mm(merged, wo_ref[...])
    o_ref[...] = h_ref[...] + _rms(out, gpost_ref[...])
    kvp_ref[...] = kvq_ref[:, 0:2 * D_ATTN]


def _mixer(kvq, act_gates, h, brow, wa, wc, wo, gpost, seq_len):
    t, d = h.shape
    tm = ROW_TILE
    tiles_per_seq = seq_len // tm
    return pl.pallas_call(
        functools.partial(_mixer_kernel, tiles_per_seq),
        grid=(t // tm,),
        in_specs=[
            _row_spec(tm, 3 * D_ATTN),
            _row_spec(tm, act_gates.shape[1]),
            _row_spec(tm, d),
            _const_spec(brow.shape),
            _const_spec(wa.shape),
            _const_spec(wc.shape),
            _const_spec(wo.shape),
            _const_spec(gpost.shape),
        ],
        out_specs=_row_spec(tm, d),
        out_shape=jax.ShapeDtypeStruct((t, d), F32),
        scratch_shapes=[
            pltpu.VMEM((N_HEADS, K_BLOCK, Q_BLOCK), F32),
            pltpu.VMEM((SCORE_SLOTS * 2 * tm // Q_BLOCK, K_BLOCK, Q_BLOCK), F32),
            pltpu.VMEM((D_ATTN, tm), BF16),
            pltpu.VMEM(wa.shape, BF16),
            pltpu.VMEM(wc.shape, BF16),
            pltpu.VMEM(wo.shape, BF16),
            pltpu.VMEM((tm, 2 * D_ATTN), BF16),
        ],
        compiler_params=_params(),
        name="mixer",
    )(kvq, act_gates, h, brow, wa, wc, wo, gpost)


def _rel_bias_rows(rel_table):
    look_back = LEFT_CHUNKS * CHUNK
    n_far = look_back + Q_BLOCK - REL_CLIP
    n_near = BIAS_ROW_LEN - n_far - (2 * REL_CLIP + 1)
    t = rel_table.astype(F32)
    hd = t.shape[0]
    row = jnp.concatenate([jnp.broadcast_to(t[:, -1:], (hd, n_far)), t[:, ::-1],
                           jnp.broadcast_to(t[:, :1], (hd, n_near))], axis=1)
    return row * LOG2E


def kernel(x, ffn1_norm_pre, ffn1_w_gate, ffn1_w_up, ffn1_w_down, ffn1_norm_post, mix_norm_pre, w_in, gate_bias, rel_table, w_attn_out, conv_glu_bias, conv_dw_w, conv_dw_b, conv_ln_g, conv_ln_b, conv_w_out, w_out, mix_norm_post, ffn2_norm_pre, ffn2_w_gate, ffn2_w_up, ffn2_w_down, ffn2_norm_post):
    b, s, d = x.shape
    assert s % ROW_TILE == 0 and d % LANES == 0
    h = x.reshape(b * s, d)
    depth = w_in.shape[0]
    row = lambda p: p.reshape(1, -1)
    for l in range(depth):
        h = _ffn(h, row(ffn1_norm_pre[l]), ffn1_w_gate[l], ffn1_w_up[l], ffn1_w_down[l],
                 row(ffn1_norm_post[l]))
        kvq, act_gates = _proj(h, row(mix_norm_pre[l]), w_in[l],
                               row(conv_glu_bias[l]), row(gate_bias[l]),
                               conv_dw_w[l].reshape(CONV_WIDTH, D_CONV), row(conv_dw_b[l]),
                               row(conv_ln_g[l]), row(conv_ln_b[l]), s)
        h = _mixer(kvq, act_gates, h, _rel_bias_rows(rel_table[l]), w_attn_out[l],
                   conv_w_out[l], w_out[l], row(mix_norm_post[l]), s)
        h = _ffn(h, row(ffn2_norm_pre[l]), ffn2_w_gate[l], ffn2_w_up[l], ffn2_w_down[l],
                 row(ffn2_norm_post[l]))
    return h.reshape(b, s, d)
```
